```python
import math, functools
import jax, jax.numpy as jnp
from jax import lax
import numpy as np

D_MODEL = 1024
BATCH = 16
SEQ = 2048
DEPTH = 2
DEC_BATCH = 32
DEC_SEQ = 4
PAST_LEN = 16384
PAGE_SIZE = 128

H_A = 4
DK_A = 128
DV_A = 128
D_A = H_A * DV_A
CONV_K = 4
CONV_DIM = 2 * H_A * DK_A + H_A * DV_A
CHUNK_A = 64
H_B = 8
D_HB = 64
D_B = H_B * D_HB
Q_BLOCK = 128
EPS = 1e-6
SPLITS = (CONV_DIM, D_A, H_A, H_A, 3 * D_B, H_B, D_B, D_MODEL, D_MODEL)
N_IN = sum(SPLITS)
SPLIT_POINTS = tuple(int(s) for s in np.cumsum(SPLITS)[:-1])

kernel_name = "hybrid_gdn_fox_gated_merge_step"


def rmsnorm(x, g):
    xf = x.astype(jnp.float32)
    y = xf * lax.rsqrt(jnp.mean(xf * xf, axis=-1, keepdims=True) + EPS)
    return y.astype(x.dtype) * g


def l2norm(x):
    return x * lax.rsqrt(jnp.sum(x * x, axis=-1, keepdims=True) + EPS)


def causal_conv(u, buf, w):
    T = u.shape[1]
    xp = jnp.concatenate([buf.astype(u.dtype), u], axis=1)
    y = sum(w[j] * xp[:, j:j + T] for j in range(CONV_K))
    return jax.nn.silu(y), xp[:, T:]


def gated_delta_rule(q, k, v, g, beta, S0):
    B, T, H, DK = q.shape
    DV = v.shape[-1]
    C = math.gcd(T, CHUNK_A)
    N = T // C

    def chunks(t):
        return jnp.moveaxis(t.reshape((B, N, C) + t.shape[2:]), 1, 0).swapaxes(2, 3)

    tril = jnp.tril(jnp.ones((C, C), bool))
    strict = jnp.tril(jnp.ones((C, C), bool), -1)
    eye = jnp.eye(C, dtype=jnp.float32)

    def body(S, xs):
        qc, kc, vc, gc, bc = xs
        gcum = jnp.cumsum(gc, axis=-1)
        diff = gcum[..., :, None] - gcum[..., None, :]
        decay = jnp.exp(jnp.where(tril, diff, -jnp.inf))
        kb = kc * bc[..., None]
        A = jnp.where(strict, jnp.einsum('bhid,bhjd->bhij', kb, kc) * decay, 0.0)
        Tm = eye + A
        u = lax.linalg.triangular_solve(Tm, vc * bc[..., None], left_side=True, lower=True,
                                        unit_diagonal=True)
        w = lax.linalg.triangular_solve(Tm, kb * jnp.exp(gcum)[..., None], left_side=True,
                                        lower=True, unit_diagonal=True)
        v_new = u - jnp.einsum('bhcd,bhde->bhce', w, S)
        attn = jnp.einsum('bhid,bhjd->bhij', qc, kc) * decay
        o = (jnp.einsum('bhcd,bhde->bhce', qc * jnp.exp(gcum)[..., None], S)
             + jnp.einsum('bhij,bhje->bhie', attn, v_new))
        glast = gcum[..., -1]
        S_new = (S * jnp.exp(glast)[..., None, None]
                 + jnp.einsum('bhcd,bhce->bhde', kc * jnp.exp(glast[..., None] - gcum)[..., None], v_new))
        return S_new, o

    S_fin, o = lax.scan(body, S0, (chunks(q), chunks(k), chunks(v), chunks(g), chunks(beta)))
    o = jnp.moveaxis(o.swapaxes(2, 3), 0, 1).reshape(B, T, H, DV)
    return o, S_fin


def fox_prompt(q, k, v, logf):
    B, T, H, D = q.shape
    Q = min(Q_BLOCK, T)
    NB = T // Q
    scale = D ** -0.5
    cT = jnp.cumsum(logf, axis=1).transpose(0, 2, 1)
    kpos = jnp.arange(T)

    def block(args):
        qb, cb, start = args
        s = jnp.einsum('bqhd,bshd->bhqs', qb, k, preferred_element_type=jnp.float32) * scale
        s = s + cb[..., :, None] - cT[:, :, None, :]
        qpos = start + jnp.arange(Q)
        s = jnp.where(kpos[None, :] <= qpos[:, None], s, -jnp.inf)
        p = jax.nn.softmax(s, axis=-1)
        return jnp.einsum('bhqs,bshd->bqhd', p.astype(v.dtype), v)

    qb = q.reshape(B, NB, Q, H, D).swapaxes(0, 1)
    cb = cT.reshape(B, H, NB, Q).transpose(2, 0, 1, 3)
    out = lax.map(block, (qb, cb, jnp.arange(NB) * Q))
    return out.swapaxes(0, 1).reshape(B, T, H, D)


def fox_sample(q, k, v, logf, *, cache_k, cache_v, cache_logf, page_table, layer):
    B, T, H, D = q.shape
    n_pages = page_table.shape[1]
    scale = D ** -0.5
    past_logf = cache_logf[layer, page_table].reshape(B, n_pages * PAGE_SIZE, H).astype(jnp.float32)
    suffix = lax.cumsum(past_logf, axis=1, reverse=True) - past_logf
    suffix_pages = suffix.reshape(B, n_pages, PAGE_SIZE, H).transpose(1, 0, 3, 2)
    c_new = jnp.cumsum(logf, axis=1).transpose(0, 2, 1)

    def page_step(carry, xs):
        m, l, acc = carry
        pid, bias = xs
        kp = cache_k[layer, pid]
        vp = cache_v[layer, pid]
        s = jnp.einsum('bthd,bshd->bhts', q, kp, preferred_element_type=jnp.float32) * scale
        s = s + bias[:, :, None, :] + c_new[..., None]
        m_new = jnp.maximum(m, jnp.max(s, axis=-1))
        corr = jnp.exp(m - m_new)
        p = jnp.exp(s - m_new[..., None])
        l = l * corr + jnp.sum(p, axis=-1)
        acc = acc * corr[..., None] + jnp.einsum('bhts,bshd->bhtd', p, vp.astype(jnp.float32))
        return (m_new, l, acc), None

    init = (jnp.full((B, H, T), -jnp.inf, jnp.float32), jnp.zeros((B, H, T), jnp.float32),
            jnp.zeros((B, H, T, D), jnp.float32))
    (m, l, acc), _ = lax.scan(page_step, init, (page_table.T, suffix_pages))
    s = jnp.einsum('bthd,bshd->bhts', q, k, preferred_element_type=jnp.float32) * scale
    s = s + c_new[..., :, None] - c_new[..., None, :]
    s = jnp.where(jnp.tril(jnp.ones((T, T), bool)), s, -jnp.inf)
    m_new = jnp.maximum(m, jnp.max(s, axis=-1))
    corr = jnp.exp(m - m_new)
    p = jnp.exp(s - m_new[..., None])
    l = l * corr + jnp.sum(p, axis=-1)
    acc = acc * corr[..., None] + jnp.einsum('bhts,bshd->bhtd', p, v.astype(jnp.float32))
    out = acc / l[..., None]
    return out.transpose(0, 2, 1, 3).astype(q.dtype)


def mixer_layer(x, conv_buf, S0, attend, norm_g, w_in, conv_w, A_log, dt_bias, onorm_g, b_f,
                w_pa, w_pb, w_o):
    B, T, _ = x.shape
    h = rmsnorm(x, norm_g)
    proj = h @ w_in
    qkv_a, z_a, beta_in, a_in, qkv_b, f_in, z_b, gate_a, gate_b = jnp.split(proj, SPLIT_POINTS, axis=-1)
    conv_out, new_conv = causal_conv(qkv_a, conv_buf, conv_w)
    qa, ka, va = jnp.split(conv_out, [H_A * DK_A, 2 * H_A * DK_A], axis=-1)
    qa = l2norm(qa.reshape(B, T, H_A, DK_A).astype(jnp.float32)) * DK_A ** -0.5
    ka = l2norm(ka.reshape(B, T, H_A, DK_A).astype(jnp.float32))
    va = va.reshape(B, T, H_A, DV_A).astype(jnp.float32)
    beta = jax.nn.sigmoid(beta_in.astype(jnp.float32))
    g = -jnp.exp(A_log.astype(jnp.float32)) * jax.nn.softplus(a_in.astype(jnp.float32)
                                                              + dt_bias.astype(jnp.float32))
    o_a, S_new = gated_delta_rule(qa, ka, va, g, beta, S0.astype(jnp.float32))
    o_a = rmsnorm(o_a, onorm_g.astype(jnp.float32)).astype(x.dtype) * jax.nn.silu(z_a).reshape(B, T, H_A, DV_A)
    y_a = o_a.reshape(B, T, D_A) @ w_pa
    qb, kb, vb = [t.reshape(B, T, H_B, D_HB) for t in jnp.split(qkv_b, 3, axis=-1)]
    logf = jax.nn.log_sigmoid((f_in + b_f).astype(jnp.float32))
    o_b = attend(qb, kb, vb, logf)
    y_b = (o_b.reshape(B, T, D_B) * jax.nn.silu(z_b)) @ w_pb
    merged = jax.nn.sigmoid(gate_a) * y_a + jax.nn.sigmoid(gate_b) * y_b
    return x + merged @ w_o, (kb, vb, logf, S_new, new_conv)


def setup_inputs(seed: int = 0) -> dict:
    key = jax.random.key(seed)
    ks = jax.random.split(key, 20)
    n_pages = PAST_LEN // PAGE_SIZE
    n_used = DEC_BATCH * n_pages
    n_pool = n_used + max(1, n_used // 4)

    def nrm(k, shape, s=1.0):
        return s * jax.random.normal(k, shape, jnp.float32)

    x_prompt = nrm(ks[0], (BATCH, SEQ, D_MODEL))
    x_sample = nrm(ks[1], (DEC_BATCH, DEC_SEQ, D_MODEL))
    cache_k = nrm(ks[2], (DEPTH, n_pool, PAGE_SIZE, H_B, D_HB))
    cache_v = nrm(ks[3], (DEPTH, n_pool, PAGE_SIZE, H_B, D_HB))
    cache_logf = jax.nn.log_sigmoid(7.0 + nrm(ks[4], (DEPTH, n_pool, PAGE_SIZE, H_B), 0.5))
    state_delta = nrm(ks[5], (DEPTH, DEC_BATCH, H_A, DK_A, DV_A), 0.5)
    state_conv = nrm(ks[6], (DEPTH, DEC_BATCH, CONV_K - 1, CONV_DIM))
    page_table = jax.random.permutation(ks[7], n_pool)[:n_used].reshape(DEC_BATCH, n_pages).astype(jnp.int32)
    norm_g = 1.0 + nrm(ks[8], (DEPTH, D_MODEL), 0.01)
    w_in = nrm(ks[9], (DEPTH, D_MODEL, N_IN), D_MODEL ** -0.5)
    conv_w = nrm(ks[10], (DEPTH, CONV_K, CONV_DIM), CONV_K ** -0.5)
    A_log = jnp.log(jax.random.uniform(ks[11], (DEPTH, H_A), jnp.float32, 1.0, 16.0))
    dt = jnp.exp(jax.random.uniform(ks[12], (DEPTH, H_A), jnp.float32, math.log(1e-3), math.log(1e-1)))
    dt_bias = dt + jnp.log(-jnp.expm1(-dt))
    onorm_g = 1.0 + nrm(ks[13], (DEPTH, DV_A), 0.01)
    b_f = 2.0 + nrm(ks[14], (DEPTH, H_B), 0.1)
    w_pa = nrm(ks[15], (DEPTH, D_A, D_MODEL), D_A ** -0.5)
    w_pb = nrm(ks[16], (DEPTH, D_B, D_MODEL), D_B ** -0.5)
    w_o = nrm(ks[17], (DEPTH, D_MODEL, D_MODEL), D_MODEL ** -0.5)
    final_g = 1.0 + nrm(ks[18], (D_MODEL,), 0.01)
    return {"x_prompt": x_prompt, "x_sample": x_sample, "cache_k": cache_k, "cache_v": cache_v,
            "cache_logf": cache_logf, "state_delta": state_delta, "state_conv": state_conv,
            "page_table": page_table, "norm_g": norm_g, "w_in": w_in, "conv_w": conv_w,
            "A_log": A_log, "dt_bias": dt_bias, "onorm_g": onorm_g, "b_f": b_f, "w_pa": w_pa,
            "w_pb": w_pb, "w_o": w_o, "final_g": final_g}


def reference(x_prompt, x_sample, cache_k, cache_v, cache_logf, state_delta, state_conv, page_table,
              norm_g, w_in, conv_w, A_log, dt_bias, onorm_g, b_f, w_pa, w_pb, w_o, final_g):
    hp, hs = x_prompt, x_sample
    st_p, st_s = [], []
    for l in range(DEPTH):
        lw = (norm_g[l], w_in[l], conv_w[l], A_log[l], dt_bias[l], onorm_g[l], b_f[l],
              w_pa[l], w_pb[l], w_o[l])
        conv0 = jnp.zeros((hp.shape[0], CONV_K - 1, CONV_DIM), hp.dtype)
        S0 = jnp.zeros((hp.shape[0], H_A, DK_A, DV_A), jnp.float32)
        hp, sp = mixer_layer(hp, conv0, S0, fox_prompt, *lw)
        attend_s = functools.partial(fox_sample, cache_k=cache_k, cache_v=cache_v, cache_logf=cache_logf,
                                     page_table=page_table, layer=l)
        hs, ss = mixer_layer(hs, state_conv[l], state_delta[l], attend_s, *lw)
        st_p.append(sp)
        st_s.append(ss)
    y_prompt = rmsnorm(hp, final_g)
    y_sample = rmsnorm(hs, final_g)
    k_prompt = jnp.stack([s[0] for s in st_p])
    v_prompt = jnp.stack([s[1] for s in st_p])
    logf_prompt = jnp.stack([s[2] for s in st_p])
    delta_prompt = jnp.stack([s[3] for s in st_p])
    conv_prompt = jnp.stack([s[4] for s in st_p])
    k_sample = jnp.stack([s[0] for s in st_s])
    v_sample = jnp.stack([s[1] for s in st_s])
    logf_sample = jnp.stack([s[2] for s in st_s])
    delta_sample = jnp.stack([s[3] for s in st_s])
    conv_sample = jnp.stack([s[4] for s in st_s])
    return (y_prompt, y_sample, k_prompt, v_prompt, logf_prompt, delta_prompt, conv_prompt,
            k_sample, v_sample, logf_sample, delta_sample, conv_sample)
```

```python
import functools

import jax
import jax.numpy as jnp
from jax import lax
from jax.experimental import pallas as pl
from jax.experimental.pallas import tpu as pltpu

F32 = jnp.float32
BF16 = jnp.bfloat16
EPS = 1e-6
CHUNK = 64
LANES = 128
SUBLANES = 8
SMALL_COLS = 16
VMEM_LIMIT = 56 * 1024 * 1024

_NT = (((1,), (1,)), ((), ()))
_TN = (((0,), (0,)), ((), ()))


def _bdot(a, b):
    return jnp.dot(a.astype(BF16), b.astype(BF16), preferred_element_type=F32)


def _bdot_nt(a, b):
    return lax.dot_general(a.astype(BF16), b.astype(BF16), _NT, preferred_element_type=F32)


def _bdot_tn(a, b):
    return lax.dot_general(a.astype(BF16), b.astype(BF16), _TN, preferred_element_type=F32)


def _split3(x):
    x1 = x.astype(BF16)
    r1 = x - x1.astype(F32)
    x2 = r1.astype(BF16)
    x3 = (r1 - x2.astype(F32)).astype(BF16)
    return x1, x2, x3


def _xdot(mask_bf16, x, dims=None):
    parts = _split3(x)
    if dims is None:
        return sum(jnp.dot(mask_bf16, p, preferred_element_type=F32) for p in parts)
    return sum(lax.dot_general(mask_bf16, p, dims, preferred_element_type=F32) for p in parts)


def _xdot_r(x, mask_bf16, dims=None):
    parts = _split3(x)
    if dims is None:
        return sum(jnp.dot(p, mask_bf16, preferred_element_type=F32) for p in parts)
    return sum(lax.dot_general(p, mask_bf16, dims, preferred_element_type=F32) for p in parts)


def _iota2(shape, dim):
    return lax.broadcasted_iota(jnp.int32, shape, dim)


def _params(sem):
    return pltpu.CompilerParams(dimension_semantics=sem, vmem_limit_bytes=VMEM_LIMIT)


def _inproj_body(x_ref, g_ref, w_ref, ws_ref, wst_ref, *out_refs, segs):
    x = x_ref[...]
    h = (x * lax.rsqrt(jnp.mean(x * x, axis=-1, keepdims=True) + EPS)) * g_ref[...]
    hb = h.astype(BF16)
    for (lo, hi), o_ref in zip(segs, out_refs[:-2]):
        o_ref[...] = jnp.dot(hb, w_ref[:, lo:hi], preferred_element_type=F32).astype(o_ref.dtype)
    sm_ref, smt_ref = out_refs[-2:]
    sm_ref[...] = jnp.dot(hb, ws_ref[...], preferred_element_type=F32)
    smt_ref[...] = lax.dot_general(wst_ref[...], hb, _NT, preferred_element_type=F32)


def _inproj(x2d, g, wbig, wsmall, wsmall_t, widths, dtypes, tm):
    m, d = x2d.shape
    offs = [0]
    for w in widths:
        offs.append(offs[-1] + w)
    segs = tuple((offs[i], offs[i + 1]) for i in range(len(widths)))
    const = lambda i: (0, 0)
    out_shape = [jax.ShapeDtypeStruct((m, w), dt) for w, dt in zip(widths, dtypes)]
    out_shape += [jax.ShapeDtypeStruct((m, LANES), F32), jax.ShapeDtypeStruct((SMALL_COLS, m), F32)]
    out_specs = [pl.BlockSpec((tm, w), lambda i: (i, 0)) for w in widths]
    out_specs += [pl.BlockSpec((tm, LANES), lambda i: (i, 0)), pl.BlockSpec((SMALL_COLS, tm), lambda i: (0, i))]
    return pl.pallas_call(
        functools.partial(_inproj_body, segs=segs),
        grid=(m // tm,),
        in_specs=[pl.BlockSpec((tm, d), lambda i: (i, 0)),
                  pl.BlockSpec((1, d), const),
                  pl.BlockSpec(wbig.shape, const, pipeline_mode=pl.Buffered(1)),
                  pl.BlockSpec(wsmall.shape, const, pipeline_mode=pl.Buffered(1)),
                  pl.BlockSpec(wsmall_t.shape, const, pipeline_mode=pl.Buffered(1))],
        out_specs=out_specs,
        out_shape=out_shape,
        compiler_params=_params(("parallel",)),
        name="inproj",
    )(x2d, g.reshape(1, d), wbig, wsmall, wsmall_t)


def _fgate_body(sm_ref, smt_ref, bfr_ref, bfc_ref, logf_ref, ccol_ref, crow_ref, carry_c, carry_r, *, nh):
    j = pl.program_id(1)
    tc = sm_ref.shape[0]

    @pl.when(j == 0)
    def _():
        carry_c[...] = jnp.zeros_like(carry_c)
        carry_r[...] = jnp.zeros_like(carry_r)

    logf_c = jax.nn.log_sigmoid(sm_ref[...] + bfr_ref[...])
    logf_ref[...] = logf_c[:, :nh]
    tril = (_iota2((tc, tc), 0) >= _iota2((tc, tc), 1)).astype(BF16)
    ccol = _xdot(tril, logf_c) + carry_c[...]
    ccol_ref[...] = ccol[:, :nh]
    carry_c[...] = ccol[tc - 1:tc, :]

    logf_r = jax.nn.log_sigmoid(smt_ref[0:nh, :] + bfc_ref[...])
    triu = (_iota2((tc, tc), 0) <= _iota2((tc, tc), 1)).astype(BF16)
    crow = _xdot_r(logf_r, triu) + carry_r[:, 0:1]
    crow_ref[...] = crow
    carry_r[...] = jnp.broadcast_to(crow[:, tc - 1:tc], carry_r.shape)


def _fgate(small, small_t, b_f, nb, t, tc):
    nh = b_f.shape[0]
    nj = t // tc
    bfr = jnp.zeros((1, LANES), F32).at[0, :nh].set(b_f)
    bfc = b_f.reshape(nh, 1)
    return pl.pallas_call(
        functools.partial(_fgate_body, nh=nh),
        grid=(nb, nj),
        in_specs=[pl.BlockSpec((tc, LANES), lambda b, j: (b * nj + j, 0)),
                  pl.BlockSpec((SMALL_COLS, tc), lambda b, j: (0, b * nj + j)),
                  pl.BlockSpec((1, LANES), lambda b, j: (0, 0)),
                  pl.BlockSpec((nh, 1), lambda b, j: (0, 0))],
        out_specs=[pl.BlockSpec((None, tc, nh), lambda b, j: (b, j, 0)),
                   pl.BlockSpec((None, tc, nh), lambda b, j: (b, j, 0)),
                   pl.BlockSpec((None, nh, tc), lambda b, j: (b, 0, j))],
        out_shape=[jax.ShapeDtypeStruct((nb, t, nh), F32),
                   jax.ShapeDtypeStruct((nb, t, nh), F32),
                   jax.ShapeDtypeStruct((nb, nh, t), F32)],
        scratch_shapes=[pltpu.VMEM((1, LANES), F32), pltpu.VMEM((nh, LANES), F32)],
        compiler_params=_params(("parallel", "arbitrary")),
        name="fgate",
    )(small, small_t, bfr, bfc)


def _delta_body(qkv_ref, sm_ref, za_ref, conv0_ref, s0_ref, cw_ref, alog_ref, dtb_ref, og_ref,
                oa_ref, sfin_ref, xbuf, s_scr, *, nh, dk, dv, conv_k, t_valid, n_chunks, beta_lane):
    j = pl.program_id(1)
    lb = qkv_ref.shape[0]
    c = CHUNK
    pad = SUBLANES

    @pl.when(j == 0)
    def _():
        xbuf[0:pad, :] = conv0_ref[...]
        s_scr[...] = s0_ref[...]

    @pl.when(j > 0)
    def _():
        xbuf[0:pad, :] = xbuf[lb:lb + pad, :]

    xbuf[pad:pad + lb, :] = qkv_ref[...]

    acc = None
    for tap in range(conv_k):
        start = pad - (conv_k - 1) + tap
        term = cw_ref[tap:tap + 1, :] * xbuf[start:start + lb, :]
        acc = term if acc is None else acc + term
    y = jax.nn.silu(acc)

    sm = sm_ref[...]
    beta_all = jax.nn.sigmoid(sm)
    g_all = -jnp.exp(alog_ref[...]) * jax.nn.softplus(sm + dtb_ref[...])
    if t_valid is not None:
        tok = j * lb + _iota2((lb, LANES), 0)
        valid = tok < t_valid
        beta_all = jnp.where(valid, beta_all, 0.0)
        g_all = jnp.where(valid, g_all, 0.0)

    ri = _iota2((c, c), 0)
    ci = _iota2((c, c), 1)
    tril_b = (ri >= ci).astype(BF16)
    lower = ri >= ci
    strict = ri > ci
    eye = (ri == ci).astype(F32)
    og = og_ref[...]
    qscale = dk ** -0.5
    kbase = nh * dk
    vbase = 2 * nh * dk
    a_lane = beta_lane + nh

    for ch in range(n_chunks):
        r0 = ch * c
        for h in range(nh):
            q = y[r0:r0 + c, h * dk:(h + 1) * dk]
            k = y[r0:r0 + c, kbase + h * dk:kbase + (h + 1) * dk]
            v = y[r0:r0 + c, vbase + h * dv:vbase + (h + 1) * dv]
            q = q * lax.rsqrt(jnp.sum(q * q, axis=-1, keepdims=True) + EPS) * qscale
            k = k * lax.rsqrt(jnp.sum(k * k, axis=-1, keepdims=True) + EPS)
            beta = beta_all[r0:r0 + c, beta_lane + h:beta_lane + h + 1]
            g = g_all[r0:r0 + c, a_lane + h:a_lane + h + 1]

            g_sq = jnp.broadcast_to(g, (c, c))
            diff = _xdot(tril_b, jnp.where(strict, g_sq, 0.0))
            gcum = _xdot(tril_b, jnp.broadcast_to(g, (c, dk)))
            decay = jnp.where(lower, jnp.exp(diff), 0.0)
            egc = jnp.exp(gcum)

            kb = k * beta
            a_mat = jnp.where(strict, _bdot_nt(kb, k) * decay, 0.0)
            n_pow = -a_mat
            t_inv = eye + n_pow
            span = 2
            while span < c:
                n_pow = _bdot(n_pow, n_pow)
                t_inv = t_inv + _bdot(t_inv, n_pow)
                span *= 2
            uw = _bdot(t_inv, jnp.concatenate([v * beta, kb * egc], axis=1))
            u = uw[:, :dv]
            w = uw[:, dv:]
            attn = _bdot_nt(q, k) * decay

            s_old = s_scr[h]
            ws = _bdot(jnp.concatenate([w, q * egc], axis=0), s_old)
            v_new = u - ws[:c]
            o = ws[c:] + _bdot(attn, v_new)
            glast = gcum[c - 1:c, :]
            kd = k * jnp.exp(glast - gcum)
            s_scr[h] = s_old * jnp.exp(glast[:, :dv]) + _bdot_tn(kd, v_new)

            o = o * lax.rsqrt(jnp.mean(o * o, axis=-1, keepdims=True) + EPS) * og
            z = za_ref[r0:r0 + c, h * dv:(h + 1) * dv]
            oa_ref[r0:r0 + c, h * dv:(h + 1) * dv] = (o * jax.nn.silu(z)).astype(oa_ref.dtype)

    @pl.when(j == pl.num_programs(1) - 1)
    def _():
        sfin_ref[...] = s_scr[...]


def _delta(qkva, small, za, conv0, s0, conv_w, a_log, dt_bias, onorm_g, nb, t, lb, t_valid, beta_lane):
    nh, dk, dv = s0.shape[1:]
    conv_k, conv_dim = conv_w.shape
    nj = t // lb
    a_lane = beta_lane + nh
    alog_row = jnp.zeros((1, LANES), F32).at[0, a_lane:a_lane + nh].set(a_log)
    dtb_row = jnp.zeros((1, LANES), F32).at[0, a_lane:a_lane + nh].set(dt_bias)
    const = lambda b, j: (0, 0)
    body = functools.partial(_delta_body, nh=nh, dk=dk, dv=dv, conv_k=conv_k, t_valid=t_valid,
                             n_chunks=lb // CHUNK, beta_lane=beta_lane)
    return pl.pallas_call(
        body,
        grid=(nb, nj),
        in_specs=[pl.BlockSpec((lb, conv_dim), lambda b, j: (b * nj + j, 0)),
                  pl.BlockSpec((lb, LANES), lambda b, j: (b * nj + j, 0)),
                  pl.BlockSpec((lb, nh * dv), lambda b, j: (b * nj + j, 0)),
                  pl.BlockSpec((None, SUBLANES, conv_dim), lambda b, j: (b, 0, 0)),
                  pl.BlockSpec((None, nh, dk, dv), lambda b, j: (b, 0, 0, 0)),
                  pl.BlockSpec((conv_k, conv_dim), const),
                  pl.BlockSpec((1, LANES), const),
                  pl.BlockSpec((1, LANES), const),
                  pl.BlockSpec((1, dv), const)],
        out_specs=[pl.BlockSpec((lb, nh * dv), lambda b, j: (b * nj + j, 0)),
                   pl.BlockSpec((None, nh, dk, dv), lambda b, j: (b, 0, 0, 0))],
        out_shape=[jax.ShapeDtypeStruct((nb * t, nh * dv), BF16),
                   jax.ShapeDtypeStruct((nb, nh, dk, dv), F32)],
        scratch_shapes=[pltpu.VMEM((lb + 2 * SUBLANES, conv_dim), F32),
                        pltpu.VMEM((nh, dk, dv), F32)],
        compiler_params=_params(("parallel", "arbitrary")),
        name="delta",
    )(qkva, small, za, conv0, s0, conv_w, alog_row, dtb_row, onorm_g.reshape(1, dv))


def _fox_body(q_ref, k_ref, v_ref, ccol_ref, crow_ref, zb_ref, ob_ref, *, nh, dh, tk):
    qi = pl.program_id(1)
    tq = q_ref.shape[0]
    scale = dh ** -0.5
    lane = _iota2((1, LANES), 1)
    per = LANES // dh
    rows = _iota2((tq, tk), 0)
    cols = _iota2((tq, tk), 1)

    for grp in range(nh // per):
        gl = slice(grp * LANES, (grp + 1) * LANES)
        q2 = q_ref[:, gl]
        outs = []
        for hh in range(per):
            h = grp * per + hh
            in_head = (lane >= hh * dh) & (lane < (hh + 1) * dh)
            qh = jnp.where(in_head, q2, jnp.zeros_like(q2))
            ct = ccol_ref[:, h:h + 1]

            def step(jb, carry, masked, qh=qh, ct=ct, h=h, gl=gl):
                m, l, acc = carry
                ks = pl.multiple_of(jb * tk, tk)
                k2 = k_ref[pl.ds(ks, tk), gl]
                v2 = v_ref[pl.ds(ks, tk), gl]
                cs = crow_ref[h:h + 1, pl.ds(ks, tk)]
                s = _bdot_nt(qh, k2) * scale + ct - cs
                if masked:
                    s = jnp.where(cols <= rows, s, -jnp.inf)
                m_new = jnp.maximum(m, jnp.max(s, axis=-1, keepdims=True))
                corr = jnp.exp(m - m_new)
                p = jnp.exp(s - m_new)
                l = l * corr + jnp.sum(p, axis=-1, keepdims=True)
                acc = acc * corr + _bdot(p, v2)
                return m_new, l, acc

            init = (jnp.full((tq, 1), -jnp.inf, F32), jnp.zeros((tq, 1), F32), jnp.zeros((tq, LANES), F32))
            carry = lax.fori_loop(0, qi, functools.partial(step, masked=False), init)
            m, l, acc = step(qi, carry, True)
            outs.append((in_head, acc / l))
        o2 = outs[0][1]
        for in_head, o in outs[1:]:
            o2 = jnp.where(in_head, o, o2)
        ob_ref[:, gl] = (o2 * jax.nn.silu(zb_ref[:, gl])).astype(ob_ref.dtype)


def _fox_prompt(qb, kb, vb, ccol, crow, zb, nb, t, nh, dh, tq):
    d_b = nh * dh
    nq = t // tq
    q3 = qb.reshape(nb, t, d_b)
    k3 = kb.reshape(nb, t, d_b)
    v3 = vb.reshape(nb, t, d_b)
    z3 = zb.reshape(nb, t, d_b)
    out = pl.pallas_call(
        functools.partial(_fox_body, nh=nh, dh=dh, tk=tq),
        grid=(nb, nq),
        in_specs=[pl.BlockSpec((None, tq, d_b), lambda b, i: (b, i, 0)),
                  pl.BlockSpec((None, t, d_b), lambda b, i: (b, 0, 0)),
                  pl.BlockSpec((None, t, d_b), lambda b, i: (b, 0, 0)),
                  pl.BlockSpec((None, tq, nh), lambda b, i: (b, i, 0)),
                  pl.BlockSpec((None, nh, t), lambda b, i: (b, 0, 0)),
                  pl.BlockSpec((None, tq, d_b), lambda b, i: (b, i, 0))],
        out_specs=pl.BlockSpec((None, tq, d_b), lambda b, i: (b, i, 0)),
        out_shape=jax.ShapeDtypeStruct((nb, t, d_b), BF16),
        compiler_params=_params(("parallel", "arbitrary")),
        name="fox_prompt",
    )(q3, k3, v3, ccol, crow, z3)
    return out.reshape(nb * t, d_b)


def _fox_sample_body(pt_ref, q_ref, kn_ref, vn_ref, crow_ref, zb_ref, *refs, nh, dh, t_new, pages):
    del pt_ref
    k_refs = refs[0:pages]
    v_refs = refs[pages:2 * pages]
    lf_refs = refs[2 * pages:3 * pages]
    ob_ref = refs[3 * pages]
    qblk, m_scr, l_scr, acc_scr, suf_scr = refs[3 * pages + 1:]
    j = pl.program_id(1)
    d_b = nh * dh
    nrow = t_new * nh
    page = k_refs[0].shape[0]
    scale = dh ** -0.5
    head_of_lane = _iota2((nh, d_b), 1) // dh
    head_mask = head_of_lane == _iota2((nh, d_b), 0)
    tp = crow_ref.shape[1]
    c_tiled = jnp.concatenate([crow_ref[...]] * t_new, axis=0)
    row_tok = _iota2((nrow, tp), 0) // nh
    lane_tok = _iota2((nrow, tp), 1)
    c_own = jnp.sum(jnp.where(lane_tok == row_tok, c_tiled, 0.0), axis=-1, keepdims=True)

    @pl.when(j == 0)
    def _():
        for t in range(t_new):
            qrow = jnp.broadcast_to(q_ref[t:t + 1, :].astype(F32), (nh, d_b))
            qblk[t * nh:(t + 1) * nh, :] = jnp.where(head_mask, qrow, 0.0)
        m_scr[...] = jnp.full(m_scr.shape, -jnp.inf, F32)
        l_scr[...] = jnp.zeros_like(l_scr)
        acc_scr[...] = jnp.zeros_like(acc_scr)
        suf_scr[...] = jnp.zeros_like(suf_scr)

    def update(s, vals):
        m_old = m_scr[...]
        m_new = jnp.maximum(m_old, jnp.max(s, axis=-1, keepdims=True))
        corr = jnp.exp(m_old - m_new)
        p = jnp.exp(s - m_new)
        l_scr[...] = l_scr[...] * corr + jnp.sum(p, axis=-1, keepdims=True)
        acc_scr[...] = acc_scr[...] * corr + _bdot(p, vals)
        m_scr[...] = m_new

    si = _iota2((page, page), 0)
    sj = _iota2((page, page), 1)
    after = (si > sj).astype(BF16)
    ones = jnp.ones((page, page), BF16)
    qb = qblk[...]
    for g in range(pages):
        lf = lf_refs[g][...]
        within = _xdot_r(lf, after, _TN)
        total = _xdot_r(lf, ones, _TN)
        bias = within + suf_scr[...]
        suf_scr[...] = suf_scr[...] + total
        s = _bdot_nt(qb, k_refs[g][...]) * scale
        s = s + jnp.concatenate([bias] * t_new, axis=0) + c_own
        update(s, v_refs[g][...])

    @pl.when(j == pl.num_programs(1) - 1)
    def _():
        s = _bdot_nt(qb, kn_ref[...]) * scale + c_own - c_tiled
        s = jnp.where(lane_tok <= row_tok, s, -jnp.inf)
        update(s, vn_ref[...])
        o = acc_scr[...] / l_scr[...]
        ob_ref[...] = jnp.zeros_like(ob_ref)
        for t in range(t_new):
            blk = jnp.where(head_mask, o[t * nh:(t + 1) * nh, :], 0.0)
            row = jnp.sum(blk, axis=0, keepdims=True)
            ob_ref[t:t + 1, :] = row * jax.nn.silu(zb_ref[t:t + 1, :])


def _fox_sample(qb, kb, vb, crow, zb, cache_k, cache_v, cache_logf, page_table, layer, nb, tp, t_new, pages):
    _, n_pool, page, nh, dh = cache_k.shape
    d_b = nh * dh
    n_pages = page_table.shape[1]
    nj = n_pages // pages
    ck = cache_k.reshape(cache_k.shape[0], n_pool, page, d_b)
    cv = cache_v.reshape(cache_v.shape[0], n_pool, page, d_b)
    q3 = qb.reshape(nb, tp, d_b)
    k3 = kb.reshape(nb, tp, d_b)
    v3 = vb.reshape(nb, tp, d_b)
    z3 = zb.reshape(nb, tp, d_b)

    def page_map(g):
        return lambda b, j, pt: (layer, pt[b, n_pages - 1 - (j * pages + g)], 0, 0)

    seq = lambda b, j, pt: (b, 0, 0)
    in_specs = [pl.BlockSpec((None, tp, d_b), seq),
                pl.BlockSpec((None, tp, d_b), seq),
                pl.BlockSpec((None, tp, d_b), seq),
                pl.BlockSpec((None, nh, tp), seq),
                pl.BlockSpec((None, tp, d_b), seq)]
    in_specs += [pl.BlockSpec((None, None, page, d_b), page_map(g)) for g in range(pages)]
    in_specs += [pl.BlockSpec((None, None, page, d_b), page_map(g)) for g in range(pages)]
    in_specs += [pl.BlockSpec((None, None, page, nh), page_map(g)) for g in range(pages)]
    nrow = t_new * nh
    grid_spec = pltpu.PrefetchScalarGridSpec(
        num_scalar_prefetch=1,
        grid=(nb, nj),
        in_specs=in_specs,
        out_specs=pl.BlockSpec((None, tp, d_b), seq),
        scratch_shapes=[pltpu.VMEM((nrow, d_b), F32),
                        pltpu.VMEM((nrow, 1), F32),
                        pltpu.VMEM((nrow, 1), F32),
                        pltpu.VMEM((nrow, d_b), F32),
                        pltpu.VMEM((nh, page), F32)],
    )
    out = pl.pallas_call(
        functools.partial(_fox_sample_body, nh=nh, dh=dh, t_new=t_new, pages=pages),
        grid_spec=grid_spec,
        out_shape=jax.ShapeDtypeStruct((nb, tp, d_b), F32),
        compiler_params=_params(("parallel", "arbitrary")),
        name="fox_sample",
    )(page_table, q3, k3, v3, crow, z3, *([ck] * pages), *([cv] * pages), *([cache_logf] * pages))
    return out.reshape(nb * tp, d_b)


def _merge_body(x_ref, oa_ref, ob_ref, ga_ref, gb_ref, wpa_ref, wpb_ref, wo_ref, fg_ref, y_ref, *, final_norm):
    ya = jnp.dot(oa_ref[...].astype(BF16), wpa_ref[...], preferred_element_type=F32)
    yb = jnp.dot(ob_ref[...].astype(BF16), wpb_ref[...], preferred_element_type=F32)
    merged = jax.nn.sigmoid(ga_ref[...]) * ya + jax.nn.sigmoid(gb_ref[...]) * yb
    y = x_ref[...] + jnp.dot(merged.astype(BF16), wo_ref[...], preferred_element_type=F32)
    if final_norm:
        y = (y * lax.rsqrt(jnp.mean(y * y, axis=-1, keepdims=True) + EPS)) * fg_ref[...]
    y_ref[...] = y


def _merge(x2d, oa, ob, ga, gb, wpa, wpb, wo, final_g, final_norm, tm):
    m, d = x2d.shape
    row = lambda w: pl.BlockSpec((tm, w), lambda i: (i, 0))
    const = lambda i: (0, 0)
    return pl.pallas_call(
        functools.partial(_merge_body, final_norm=final_norm),
        grid=(m // tm,),
        in_specs=[row(d), row(oa.shape[1]), row(ob.shape[1]), row(d), row(d),
                  pl.BlockSpec(wpa.shape, const), pl.BlockSpec(wpb.shape, const), pl.BlockSpec(wo.shape, const),
                  pl.BlockSpec((1, d), const)],
        out_specs=row(d),
        out_shape=jax.ShapeDtypeStruct((m, d), F32),
        compiler_params=_params(("parallel",)),
        name="merge",
    )(x2d, oa, ob, ga, gb, wpa, wpb, wo, final_g.reshape(1, d))


def _pack_in_weights(w_in_l, conv_dim, d_a, nh_a, d_b, nh_b, d_model):
    sizes = (conv_dim, d_a, nh_a, nh_a, 3 * d_b, nh_b, d_b, d_model, d_model)
    offs = [0]
    for s in sizes:
        offs.append(offs[-1] + s)
    col = lambda i: w_in_l[:, offs[i]:offs[i + 1]]
    wbig = jnp.concatenate([col(0), col(1), col(4), col(6), col(7), col(8)], axis=1).astype(BF16)
    small = jnp.concatenate([col(5), col(2), col(3)], axis=1)
    wsmall = jnp.pad(small, ((0, 0), (0, LANES - small.shape[1]))).astype(BF16)
    wsmall_t = small.T.astype(BF16)
    return wbig, wsmall, wsmall_t


def _layer(x2d, nb, t, t_valid, conv0, s0, attend, lw, final_g, final_norm, tm, lb, tc):
    (norm_g, w_in, conv_w, a_log, dt_bias, onorm_g, b_f, w_pa, w_pb, w_o) = lw
    d_model = x2d.shape[1]
    nh_a, dk, dv = s0.shape[1:]
    conv_dim = conv_w.shape[1]
    d_a = nh_a * dv
    nh_b = b_f.shape[0]
    d_b = w_pb.shape[0]
    wbig, wsmall, wsmall_t = _pack_in_weights(w_in, conv_dim, d_a, nh_a, d_b, nh_b, d_model)
    widths = (conv_dim, d_a, d_b, d_b, d_b, d_b, d_model, d_model)
    dtypes = (F32, F32, BF16, F32, F32, F32, F32, F32)
    qkva, za, qb, kb, vb, zb, ga, gb, small, small_t = _inproj(x2d, norm_g, wbig, wsmall, wsmall_t, widths, dtypes, tm)
    logf, ccol, crow = _fgate(small, small_t, b_f, nb, t, tc)
    oa, s_fin = _delta(qkva, small, za, conv0, s0, conv_w, a_log, dt_bias, onorm_g, nb, t, lb, t_valid, nh_b)
    ob = attend(qb, kb, vb, ccol, crow, zb)
    y = _merge(x2d, oa, ob, ga, gb, w_pa.astype(BF16), w_pb.astype(BF16), w_o.astype(BF16), final_g, final_norm, tm)
    return y, (kb, vb, logf, s_fin, qkva)


def kernel(x_prompt, x_sample, cache_k, cache_v, cache_logf, state_delta, state_conv, page_table, norm_g, w_in,
           conv_w, A_log, dt_bias, onorm_g, b_f, w_pa, w_pb, w_o, final_g):
    depth = norm_g.shape[0]
    nbp, tp_len, d_model = x_prompt.shape
    nbs, ts_len, _ = x_sample.shape
    nh_a, dk, dv = state_delta.shape[2:]
    conv_k, conv_dim = conv_w.shape[1:]
    nh_b, dh_b = cache_k.shape[3:]
    d_b = nh_b * dh_b
    ts_pad = LANES
    assert ts_len <= CHUNK and tp_len % LANES == 0 and conv_k - 1 <= SUBLANES and ts_len >= conv_k - 1

    hp = x_prompt.reshape(nbp * tp_len, d_model)
    hs = jnp.pad(x_sample, ((0, 0), (0, ts_pad - ts_len), (0, 0))).reshape(nbs * ts_pad, d_model)
    conv0_p = jnp.zeros((nbp, SUBLANES, conv_dim), F32)
    s0_p = jnp.zeros((nbp, nh_a, dk, dv), F32)
    n_pages = page_table.shape[1]
    pages = 8 if n_pages % 8 == 0 else 1
    tq = 256 if tp_len % 256 == 0 else LANES
    lb_p = LANES
    tm_p = 256 if (nbp * tp_len) % 256 == 0 else LANES
    tm_s = 256 if (nbs * ts_pad) % 256 == 0 else LANES
    tc_p = 512 if tp_len % 512 == 0 else LANES

    st_p, st_s = [], []
    for l in range(depth):
        lw = (norm_g[l], w_in[l], conv_w[l], A_log[l], dt_bias[l], onorm_g[l], b_f[l], w_pa[l], w_pb[l], w_o[l])
        last = l == depth - 1
        attend_p = functools.partial(_fox_prompt, nb=nbp, t=tp_len, nh=nh_b, dh=dh_b, tq=tq)
        hp, sp = _layer(hp, nbp, tp_len, None, conv0_p, s0_p, attend_p, lw, final_g, last, tm_p, lb_p, tc_p)

        conv0_s = jnp.pad(state_conv[l], ((0, 0), (SUBLANES - (conv_k - 1), 0), (0, 0)))

        def attend_s(qb, kb, vb, ccol, crow, zb, l=l):
            del ccol
            return _fox_sample(qb, kb, vb, crow, zb, cache_k, cache_v, cache_logf, page_table, l, nbs, ts_pad,
                               ts_len, pages)

        hs, ss = _layer(hs, nbs, ts_pad, ts_len, conv0_s, state_delta[l], attend_s, lw, final_g, last,
                        tm_s, ts_pad, ts_pad)
        st_p.append(sp)
        st_s.append(ss)

    def seq_view(a, nb, t, keep):
        return a.reshape(nb, t, *a.shape[1:])[:, :keep]

    y_prompt = hp.reshape(nbp, tp_len, d_model)
    y_sample = seq_view(hs, nbs, ts_pad, ts_len)
    outs_p, outs_s = [], []
    for (st, nb, t, keep, outs) in ((st_p, nbp, tp_len, tp_len, outs_p), (st_s, nbs, ts_pad, ts_len, outs_s)):
        k_all = jnp.stack([seq_view(s[0], nb, t, keep).reshape(nb, keep, nh_b, dh_b) for s in st])
        v_all = jnp.stack([seq_view(s[1], nb, t, keep).reshape(nb, keep, nh_b, dh_b) for s in st])
        logf_all = jnp.stack([s[2][:, :keep] for s in st])
        delta_all = jnp.stack([s[3] for s in st])
        conv_all = jnp.stack([seq_view(s[4], nb, t, keep)[:, keep - (conv_k - 1):] for s in st])
        outs.extend([k_all, v_all, logf_all, delta_all, conv_all])
    return (y_prompt, y_sample, *outs_p, *outs_s)
```

```python
import functools
import math

import jax
import jax.numpy as jnp
from jax import lax
from jax.experimental import pallas as pl
from jax.experimental.pallas import tpu as pltpu

F32 = jnp.float32
BF16 = jnp.bfloat16
EPS = 1e-6
CHUNK = 64
LANES = 128
SUBLANES = 8
SMALL_COLS = 16
VMEM_LIMIT = 56 * 1024 * 1024
FOX_HEADS_PER_PASS = 4
FOX_TK = 256

_NT = (((1,), (1,)), ((), ()))
_TN = (((0,), (0,)), ((), ()))


def _bdot(a, b):
    return jnp.dot(a.astype(BF16), b.astype(BF16), preferred_element_type=F32)


def _bdot_nt(a, b):
    return lax.dot_general(a.astype(BF16), b.astype(BF16), _NT, preferred_element_type=F32)


def _bdot_tn(a, b):
    return lax.dot_general(a.astype(BF16), b.astype(BF16), _TN, preferred_element_type=F32)


def _split3(x):
    x1 = x.astype(BF16)
    r1 = x - x1.astype(F32)
    x2 = r1.astype(BF16)
    x3 = (r1 - x2.astype(F32)).astype(BF16)
    return x1, x2, x3


def _xdot(mask_bf16, x, dims=None):
    parts = _split3(x)
    if dims is None:
        return sum(jnp.dot(mask_bf16, p, preferred_element_type=F32) for p in parts)
    return sum(lax.dot_general(mask_bf16, p, dims, preferred_element_type=F32) for p in parts)


def _xdot_r(x, mask_bf16, dims=None):
    parts = _split3(x)
    if dims is None:
        return sum(jnp.dot(p, mask_bf16, preferred_element_type=F32) for p in parts)
    return sum(lax.dot_general(p, mask_bf16, dims, preferred_element_type=F32) for p in parts)


def _iota2(shape, dim):
    return lax.broadcasted_iota(jnp.int32, shape, dim)


def _params(sem):
    return pltpu.CompilerParams(dimension_semantics=sem, vmem_limit_bytes=VMEM_LIMIT)


def _inproj_body(x_ref, g_ref, w_ref, ws_ref, wst_ref, *out_refs, segs):
    x = x_ref[...]
    h = (x * lax.rsqrt(jnp.mean(x * x, axis=-1, keepdims=True) + EPS)) * g_ref[...]
    hb = h.astype(BF16)
    for (lo, hi), o_ref in zip(segs, out_refs[:-2]):
        o_ref[...] = jnp.dot(hb, w_ref[:, lo:hi], preferred_element_type=F32).astype(o_ref.dtype)
    sm_ref, smt_ref = out_refs[-2:]
    sm_ref[...] = jnp.dot(hb, ws_ref[...], preferred_element_type=F32)
    smt_ref[...] = lax.dot_general(wst_ref[...], hb, _NT, preferred_element_type=F32)


def _inproj(x2d, g, wbig, wsmall, wsmall_t, widths, dtypes, tm):
    m, d = x2d.shape
    offs = [0]
    for w in widths:
        offs.append(offs[-1] + w)
    segs = tuple((offs[i], offs[i + 1]) for i in range(len(widths)))
    const = lambda i: (0, 0)
    out_shape = [jax.ShapeDtypeStruct((m, w), dt) for w, dt in zip(widths, dtypes)]
    out_shape += [jax.ShapeDtypeStruct((m, LANES), F32), jax.ShapeDtypeStruct((SMALL_COLS, m), F32)]
    out_specs = [pl.BlockSpec((tm, w), lambda i: (i, 0)) for w in widths]
    out_specs += [pl.BlockSpec((tm, LANES), lambda i: (i, 0)), pl.BlockSpec((SMALL_COLS, tm), lambda i: (0, i))]
    return pl.pallas_call(
        functools.partial(_inproj_body, segs=segs),
        grid=(m // tm,),
        in_specs=[pl.BlockSpec((tm, d), lambda i: (i, 0)),
                  pl.BlockSpec((1, d), const),
                  pl.BlockSpec(wbig.shape, const, pipeline_mode=pl.Buffered(1)),
                  pl.BlockSpec(wsmall.shape, const, pipeline_mode=pl.Buffered(1)),
                  pl.BlockSpec(wsmall_t.shape, const, pipeline_mode=pl.Buffered(1))],
        out_specs=out_specs,
        out_shape=out_shape,
        compiler_params=_params(("parallel",)),
        name="inproj",
    )(x2d, g.reshape(1, d), wbig, wsmall, wsmall_t)


def _fgate_body(sm_ref, smt_ref, bfr_ref, bfc_ref, logf_ref, ccol_ref, crow_ref, carry_c, carry_r, *, nh):
    j = pl.program_id(1)
    tc = sm_ref.shape[0]

    @pl.when(j == 0)
    def _():
        carry_c[...] = jnp.zeros_like(carry_c)
        carry_r[...] = jnp.zeros_like(carry_r)

    logf_c = jax.nn.log_sigmoid(sm_ref[...] + bfr_ref[...])
    logf_ref[...] = logf_c[:, :nh]
    tril = (_iota2((tc, tc), 0) >= _iota2((tc, tc), 1)).astype(BF16)
    ccol = _xdot(tril, logf_c) + carry_c[...]
    ccol_ref[...] = ccol[:, :nh]
    carry_c[...] = ccol[tc - 1:tc, :]

    logf_r = jax.nn.log_sigmoid(smt_ref[0:nh, :] + bfc_ref[...])
    triu = (_iota2((tc, tc), 0) <= _iota2((tc, tc), 1)).astype(BF16)
    crow = _xdot_r(logf_r, triu) + carry_r[:, 0:1]
    crow_ref[...] = crow
    carry_r[...] = jnp.broadcast_to(crow[:, tc - 1:tc], carry_r.shape)


def _fgate(small, small_t, b_f, nb, t, tc):
    nh = b_f.shape[0]
    nj = t // tc
    bfr = jnp.zeros((1, LANES), F32).at[0, :nh].set(b_f)
    bfc = b_f.reshape(nh, 1)
    return pl.pallas_call(
        functools.partial(_fgate_body, nh=nh),
        grid=(nb, nj),
        in_specs=[pl.BlockSpec((tc, LANES), lambda b, j: (b * nj + j, 0)),
                  pl.BlockSpec((SMALL_COLS, tc), lambda b, j: (0, b * nj + j)),
                  pl.BlockSpec((1, LANES), lambda b, j: (0, 0)),
                  pl.BlockSpec((nh, 1), lambda b, j: (0, 0))],
        out_specs=[pl.BlockSpec((None, tc, nh), lambda b, j: (b, j, 0)),
                   pl.BlockSpec((None, tc, nh), lambda b, j: (b, j, 0)),
                   pl.BlockSpec((None, nh, tc), lambda b, j: (b, 0, j))],
        out_shape=[jax.ShapeDtypeStruct((nb, t, nh), F32),
                   jax.ShapeDtypeStruct((nb, t, nh), F32),
                   jax.ShapeDtypeStruct((nb, nh, t), F32)],
        scratch_shapes=[pltpu.VMEM((1, LANES), F32), pltpu.VMEM((nh, LANES), F32)],
        compiler_params=_params(("parallel", "arbitrary")),
        name="fgate",
    )(small, small_t, bfr, bfc)


def _delta_body(qkv_ref, sm_ref, za_ref, conv0_ref, s0_ref, cw_ref, alog_ref, dtb_ref, og_ref,
                oa_ref, sfin_ref, xbuf, s_scr, *, nh, dk, dv, conv_k, t_valid, n_chunks, beta_lane):
    j = pl.program_id(1)
    lb = qkv_ref.shape[0]
    c = CHUNK
    pad = SUBLANES

    @pl.when(j == 0)
    def _():
        xbuf[0:pad, :] = conv0_ref[...]
        s_scr[...] = s0_ref[...]

    @pl.when(j > 0)
    def _():
        xbuf[0:pad, :] = xbuf[lb:lb + pad, :]

    xbuf[pad:pad + lb, :] = qkv_ref[...]

    acc = None
    for tap in range(conv_k):
        start = pad - (conv_k - 1) + tap
        term = cw_ref[tap:tap + 1, :] * xbuf[start:start + lb, :]
        acc = term if acc is None else acc + term
    y = jax.nn.silu(acc)

    sm = sm_ref[...]
    beta_all = jax.nn.sigmoid(sm)
    g_all = -jnp.exp(alog_ref[...]) * jax.nn.softplus(sm + dtb_ref[...])
    if t_valid is not None:
        tok = j * lb + _iota2((lb, LANES), 0)
        valid = tok < t_valid
        beta_all = jnp.where(valid, beta_all, 0.0)
        g_all = jnp.where(valid, g_all, 0.0)

    ri = _iota2((c, c), 0)
    ci = _iota2((c, c), 1)
    tril_b = (ri >= ci).astype(BF16)
    lower = ri >= ci
    strict = ri > ci
    eye = (ri == ci).astype(F32)
    og = og_ref[...]
    qscale = dk ** -0.5
    kbase = nh * dk
    vbase = 2 * nh * dk
    a_lane = beta_lane + nh

    probs = [(ch, h) for ch in range(n_chunks) for h in range(nh)]
    strict_w = _iota2((c, LANES), 0) > _iota2((c, LANES), 1)
    qs, ks, vs, betas = {}, {}, {}, {}
    diffs, gcums = {}, {}
    for ch in range(n_chunks):
        r0 = ch * c
        cols = []
        for h in range(nh):
            q = y[r0:r0 + c, h * dk:(h + 1) * dk]
            k = y[r0:r0 + c, kbase + h * dk:kbase + (h + 1) * dk]
            qs[ch, h] = q * lax.rsqrt(jnp.sum(q * q, axis=-1, keepdims=True) + EPS) * qscale
            ks[ch, h] = k * lax.rsqrt(jnp.sum(k * k, axis=-1, keepdims=True) + EPS)
            vs[ch, h] = y[r0:r0 + c, vbase + h * dv:vbase + (h + 1) * dv]
            betas[ch, h] = beta_all[r0:r0 + c, beta_lane + h:beta_lane + h + 1]
            g = g_all[r0:r0 + c, a_lane + h:a_lane + h + 1]
            g_w = jnp.broadcast_to(g, (c, LANES))
            cols += [jnp.where(strict_w, g_w, 0.0), g_w]
        cum = _xdot(tril_b, jnp.concatenate(cols, axis=1))
        for h in range(nh):
            diffs[ch, h] = cum[:, 2 * h * LANES:2 * h * LANES + c]
            gcums[ch, h] = cum[:, (2 * h + 1) * LANES:(2 * h + 2) * LANES]

    decays, egcs, kbs, n_pows, t_invs = {}, {}, {}, {}, {}
    for p in probs:
        decays[p] = jnp.where(lower, jnp.exp(diffs[p]), 0.0)
        egcs[p] = jnp.exp(gcums[p])
        kbs[p] = ks[p] * betas[p]
    for p in probs:
        n_pows[p] = -jnp.where(strict, _bdot_nt(kbs[p], ks[p]) * decays[p], 0.0)
        t_invs[p] = eye + n_pows[p]
    span = 2
    while span < c:
        for p in probs:
            n_pows[p] = _bdot(n_pows[p], n_pows[p])
        for p in probs:
            t_invs[p] = t_invs[p] + _bdot(t_invs[p], n_pows[p])
        span *= 2
    us, ws, attns = {}, {}, {}
    for p in probs:
        uw = _bdot(t_invs[p], jnp.concatenate([vs[p] * betas[p], kbs[p] * egcs[p]], axis=1))
        us[p] = uw[:, :dv]
        ws[p] = uw[:, dv:]
    for p in probs:
        attns[p] = _bdot_nt(qs[p], ks[p]) * decays[p]

    for ch in range(n_chunks):
        r0 = ch * c
        s_olds = [s_scr[h] for h in range(nh)]
        wss = [_bdot(jnp.concatenate([ws[ch, h], qs[ch, h] * egcs[ch, h]], axis=0), s_olds[h]) for h in range(nh)]
        v_news = [us[ch, h] - wss[h][:c] for h in range(nh)]
        os_ = [wss[h][c:] + _bdot(attns[ch, h], v_news[h]) for h in range(nh)]
        for h in range(nh):
            gcum = gcums[ch, h]
            glast = gcum[c - 1:c, :]
            kd = ks[ch, h] * jnp.exp(glast - gcum)
            s_scr[h] = s_olds[h] * jnp.exp(glast[:, :dv]) + _bdot_tn(kd, v_news[h])
        for h in range(nh):
            o = os_[h]
            o = o * lax.rsqrt(jnp.mean(o * o, axis=-1, keepdims=True) + EPS) * og
            z = za_ref[r0:r0 + c, h * dv:(h + 1) * dv]
            oa_ref[r0:r0 + c, h * dv:(h + 1) * dv] = (o * jax.nn.silu(z)).astype(oa_ref.dtype)

    @pl.when(j == pl.num_programs(1) - 1)
    def _():
        sfin_ref[...] = s_scr[...]


def _delta(qkva, small, za, conv0, s0, conv_w, a_log, dt_bias, onorm_g, nb, t, lb, t_valid, beta_lane):
    nh, dk, dv = s0.shape[1:]
    conv_k, conv_dim = conv_w.shape
    nj = t // lb
    a_lane = beta_lane + nh
    alog_row = jnp.zeros((1, LANES), F32).at[0, a_lane:a_lane + nh].set(a_log)
    dtb_row = jnp.zeros((1, LANES), F32).at[0, a_lane:a_lane + nh].set(dt_bias)
    const = lambda b, j: (0, 0)
    body = functools.partial(_delta_body, nh=nh, dk=dk, dv=dv, conv_k=conv_k, t_valid=t_valid,
                             n_chunks=lb // CHUNK, beta_lane=beta_lane)
    return pl.pallas_call(
        body,
        grid=(nb, nj),
        in_specs=[pl.BlockSpec((lb, conv_dim), lambda b, j: (b * nj + j, 0)),
                  pl.BlockSpec((lb, LANES), lambda b, j: (b * nj + j, 0)),
                  pl.BlockSpec((lb, nh * dv), lambda b, j: (b * nj + j, 0)),
                  pl.BlockSpec((None, SUBLANES, conv_dim), lambda b, j: (b, 0, 0)),
                  pl.BlockSpec((None, nh, dk, dv), lambda b, j: (b, 0, 0, 0)),
                  pl.BlockSpec((conv_k, conv_dim), const),
                  pl.BlockSpec((1, LANES), const),
                  pl.BlockSpec((1, LANES), const),
                  pl.BlockSpec((1, dv), const)],
        out_specs=[pl.BlockSpec((lb, nh * dv), lambda b, j: (b * nj + j, 0)),
                   pl.BlockSpec((None, nh, dk, dv), lambda b, j: (b, 0, 0, 0))],
        out_shape=[jax.ShapeDtypeStruct((nb * t, nh * dv), BF16),
                   jax.ShapeDtypeStruct((nb, nh, dk, dv), F32)],
        scratch_shapes=[pltpu.VMEM((lb + 2 * SUBLANES, conv_dim), F32),
                        pltpu.VMEM((nh, dk, dv), F32)],
        compiler_params=_params(("parallel", "arbitrary")),
        name="delta",
    )(qkva, small, za, conv0, s0, conv_w, alog_row, dtb_row, onorm_g.reshape(1, dv))


def _is_pow2(x):
    return math.frexp(x)[0] == 0.5


def _fox_body(q_ref, k_ref, v_ref, ccol_ref, crow_ref, zb_ref, ob_ref, *, nh, dh, tk, heads_per_pass):
    qi = pl.program_id(1)
    tq = q_ref.shape[0]
    scale = dh ** -0.5
    pre_scale = _is_pow2(scale)
    lane = _iota2((1, LANES), 1)
    per = LANES // dh
    rows = _iota2((tq, tk), 0)
    cols = _iota2((tq, tk), 1)

    for h0 in range(0, nh, heads_per_pass):
        heads = list(range(h0, h0 + heads_per_pass))
        gls, in_heads, qhs, cts = [], [], [], []
        for h in heads:
            grp, hh = divmod(h, per)
            gl = slice(grp * LANES, (grp + 1) * LANES)
            in_head = (lane >= hh * dh) & (lane < (hh + 1) * dh)
            q2 = q_ref[:, gl]
            if pre_scale:
                q2 = q2 * jnp.asarray(scale, q2.dtype)
            gls.append(gl)
            in_heads.append(in_head)
            qhs.append(jnp.where(in_head, q2, jnp.zeros_like(q2)))
            cts.append(ccol_ref[:, h:h + 1])

        def step(jb, carry, diag=None):
            masked = diag is not None
            ks = pl.multiple_of(jb * tk, tk)
            ss = []
            for i, h in enumerate(heads):
                s = _bdot_nt(qhs[i], k_ref[pl.ds(ks, tk), gls[i]])
                if not pre_scale:
                    s = s * scale
                ss.append(s)
            ps, new = [], []
            for i, h in enumerate(heads):
                m, l, acc = carry[i]
                cs = crow_ref[h:h + 1, pl.ds(ks, tk)]
                s = ss[i] + cts[i] - cs
                if masked:
                    s = jnp.where(cols + diag * tk <= rows, s, -jnp.inf)
                m_new = jnp.maximum(m, jnp.max(s, axis=-1, keepdims=True))
                corr = jnp.exp(m - m_new)
                p = jnp.exp(s - m_new)
                l = l * corr + jnp.sum(p, axis=-1, keepdims=True)
                ps.append(p.astype(BF16))
                new.append((m_new, l, acc * corr))
            out = []
            for i, h in enumerate(heads):
                m_new, l, acc = new[i]
                pv = jnp.dot(ps[i], v_ref[pl.ds(ks, tk), gls[i]].astype(BF16), preferred_element_type=F32)
                out.append((m_new, l, acc + pv))
            return tuple(out)

        init = tuple((jnp.full((tq, 1), -jnp.inf, F32), jnp.zeros((tq, 1), F32), jnp.zeros((tq, LANES), F32))
                     for _ in heads)
        per_q = tq // tk
        fin = lax.fori_loop(0, qi * per_q, step, init)
        for d in range(per_q):
            fin = step(qi * per_q + d, fin, diag=d)
        for g0 in range(0, heads_per_pass, per):
            o2 = None
            for i in range(g0, g0 + per):
                _, l, acc = fin[i]
                o = acc / l
                o2 = o if o2 is None else jnp.where(in_heads[i], o, o2)
            gl = gls[g0]
            ob_ref[:, gl] = (o2 * jax.nn.silu(zb_ref[:, gl])).astype(ob_ref.dtype)


def _fox_prompt(qb, kb, vb, ccol, crow, zb, nb, t, nh, dh, tq):
    d_b = nh * dh
    nq = t // tq
    q3 = qb.reshape(nb, t, d_b)
    k3 = kb.reshape(nb, t, d_b)
    v3 = vb.reshape(nb, t, d_b)
    z3 = zb.reshape(nb, t, d_b)
    out = pl.pallas_call(
        functools.partial(_fox_body, nh=nh, dh=dh, tk=min(FOX_TK, tq), heads_per_pass=FOX_HEADS_PER_PASS),
        grid=(nb, nq),
        in_specs=[pl.BlockSpec((None, tq, d_b), lambda b, i: (b, i, 0)),
                  pl.BlockSpec((None, t, d_b), lambda b, i: (b, 0, 0)),
                  pl.BlockSpec((None, t, d_b), lambda b, i: (b, 0, 0)),
                  pl.BlockSpec((None, tq, nh), lambda b, i: (b, i, 0)),
                  pl.BlockSpec((None, nh, t), lambda b, i: (b, 0, 0)),
                  pl.BlockSpec((None, tq, d_b), lambda b, i: (b, i, 0))],
        out_specs=pl.BlockSpec((None, tq, d_b), lambda b, i: (b, i, 0)),
        out_shape=jax.ShapeDtypeStruct((nb, t, d_b), BF16),
        compiler_params=_params(("parallel", "arbitrary")),
        name="fox_prompt",
    )(q3, k3, v3, ccol, crow, z3)
    return out.reshape(nb * t, d_b)


def _fox_sample_body(pt_ref, q_ref, kn_ref, vn_ref, cflat_ref, *refs, nh, dh, t_new, pages):
    del pt_ref
    k_refs = refs[0:pages]
    v_refs = refs[pages:2 * pages]
    lf_refs = refs[2 * pages:3 * pages]
    ob_ref = refs[3 * pages]
    m_scr, l_scr, acc_scr, carry_scr = refs[3 * pages + 1:]
    j = pl.program_id(1)
    nrow = t_new * nh
    page = k_refs[0].shape[0]
    fr = lf_refs[0].shape[0]
    tiles = page * nh // LANES
    scale = dh ** -0.5
    same_head = (_iota2((nrow, LANES), 1) % nh) == (_iota2((nrow, LANES), 0) % nh)
    cflat = cflat_ref[...]
    own = _iota2((nrow, LANES), 1) == _iota2((nrow, LANES), 0)
    c_own = jnp.sum(jnp.where(own, jnp.broadcast_to(cflat, (nrow, LANES)), 0.0), axis=-1, keepdims=True)
    qa = q_ref[...].reshape(nrow, dh)

    @pl.when(j == 0)
    def _():
        m_scr[...] = jnp.full(m_scr.shape, -jnp.inf, F32)
        l_scr[...] = jnp.zeros_like(l_scr)
        acc_scr[...] = jnp.zeros_like(acc_scr)
        carry_scr[...] = jnp.zeros_like(carry_scr)

    def update(s_tiles, v_flats):
        flat = [t for ts in s_tiles for t in ts]
        tmax = functools.reduce(jnp.maximum, flat)
        m_old = m_scr[...]
        m_new = jnp.maximum(m_old, jnp.max(tmax, axis=-1, keepdims=True))
        corr = jnp.exp(m_old - m_new)
        p_tiles = [[jnp.exp(t - m_new) for t in ts] for ts in s_tiles]
        psum = functools.reduce(lambda a, b: a + b, [t for ts in p_tiles for t in ts])
        l_scr[...] = l_scr[...] * corr + jnp.sum(psum, axis=-1, keepdims=True)
        pv = None
        for ts, vf in zip(p_tiles, v_flats):
            term = _bdot(jnp.concatenate(ts, axis=1) if len(ts) > 1 else ts[0], vf)
            pv = term if pv is None else pv + term
        acc_scr[...] = acc_scr[...] * corr + pv
        m_scr[...] = m_new

    ci = _iota2((LANES, LANES), 0)
    cj = _iota2((LANES, LANES), 1)
    same = (ci % nh) == (cj % nh)
    wt = jnp.concatenate([(same & (ci > cj)).astype(BF16), same.astype(BF16)], axis=1)
    n_r = pages * fr
    ui = _iota2((n_r, n_r), 0)
    uj = _iota2((n_r, n_r), 1)
    later = ((uj // fr) < (ui // fr)) | (((uj // fr) == (ui // fr)) & (uj % fr > ui % fr))
    x = jnp.concatenate([lf_refs[g][...] for g in range(pages)], axis=0)
    xw = _xdot_r(x, wt)
    rowtot = xw[:, LANES:]
    suffix = xw[:, :LANES] + _xdot(later.astype(BF16), rowtot) + carry_scr[...]
    carry_scr[...] = carry_scr[...] + jnp.sum(rowtot, axis=0, keepdims=True)

    raw = [_bdot_nt(qa, k_refs[g][...].reshape(page * nh, dh)) * scale for g in range(pages)]
    s_tiles = []
    for g in range(pages):
        ts = []
        for r in range(tiles):
            bias = suffix[g * fr + r:g * fr + r + 1, :] + c_own
            ts.append(jnp.where(same_head, raw[g][:, r * LANES:(r + 1) * LANES] + bias, -jnp.inf))
        s_tiles.append(ts)
    update(s_tiles, [v_refs[g][...].reshape(page * nh, dh) for g in range(pages)])

    @pl.when(j == pl.num_programs(1) - 1)
    def _():
        n_self = kn_ref.shape[0] * nh
        s = _bdot_nt(qa, kn_ref[...].reshape(n_self, dh)) * scale
        lane = _iota2((nrow, n_self), 1)
        row = _iota2((nrow, n_self), 0)
        keep = ((lane % nh) == (row % nh)) & ((lane // nh) <= (row // nh))
        s = jnp.where(keep, s + c_own - cflat[:, :n_self], -jnp.inf)
        update([[s]], [vn_ref[...].reshape(n_self, dh)])
        o = acc_scr[...] / l_scr[...]
        ob_ref[...] = jnp.zeros_like(ob_ref)
        ob_ref[0:t_new] = o.reshape(t_new, nh, dh)


def _fox_sample(qb, kb, vb, ccol, cache_k, cache_v, logf_flat, page_table, layer, nb, tp, t_new, pages):
    _, n_pool, page, nh, dh = cache_k.shape
    fr = logf_flat.shape[2]
    n_pages = page_table.shape[1]
    nj = n_pages // pages
    t_self = SUBLANES
    assert t_new <= t_self and t_self * nh <= LANES and page * nh == fr * LANES
    q4 = qb.astype(F32).reshape(nb, tp, nh, dh)
    k4 = kb.reshape(nb, tp, nh, dh)
    v4 = vb.reshape(nb, tp, nh, dh)
    cflat = ccol[:, :LANES // nh, :].reshape(nb, 1, LANES)

    def page_map(g):
        return lambda b, j, pt: (layer, pt[b, n_pages - 1 - (j * pages + g)], 0, 0, 0)

    def flat_map(g):
        return lambda b, j, pt: (layer, pt[b, n_pages - 1 - (j * pages + g)], 0, 0)

    seq4 = lambda b, j, pt: (b, 0, 0, 0)
    in_specs = [pl.BlockSpec((None, t_new, nh, dh), seq4),
                pl.BlockSpec((None, t_self, nh, dh), seq4),
                pl.BlockSpec((None, t_self, nh, dh), seq4),
                pl.BlockSpec((None, 1, LANES), lambda b, j, pt: (b, 0, 0))]
    in_specs += [pl.BlockSpec((None, None, page, nh, dh), page_map(g)) for g in range(pages)]
    in_specs += [pl.BlockSpec((None, None, page, nh, dh), page_map(g)) for g in range(pages)]
    in_specs += [pl.BlockSpec((None, None, fr, LANES), flat_map(g)) for g in range(pages)]
    nrow = t_new * nh
    grid_spec = pltpu.PrefetchScalarGridSpec(
        num_scalar_prefetch=1,
        grid=(nb, nj),
        in_specs=in_specs,
        out_specs=pl.BlockSpec((None, tp, nh, dh), seq4),
        scratch_shapes=[pltpu.VMEM((nrow, 1), F32),
                        pltpu.VMEM((nrow, 1), F32),
                        pltpu.VMEM((nrow, dh), F32),
                        pltpu.VMEM((1, LANES), F32)],
    )
    out = pl.pallas_call(
        functools.partial(_fox_sample_body, nh=nh, dh=dh, t_new=t_new, pages=pages),
        grid_spec=grid_spec,
        out_shape=jax.ShapeDtypeStruct((nb, tp, nh, dh), F32),
        compiler_params=_params(("parallel", "arbitrary")),
        name="fox_sample",
    )(page_table, q4, k4, v4, cflat, *([cache_k] * pages), *([cache_v] * pages), *([logf_flat] * pages))
    return out.reshape(nb * tp, nh * dh)


def _merge_body(x_ref, oa_ref, ob_ref, ga_ref, gb_ref, wpa_ref, wpb_ref, wo_ref, fg_ref, *rest, final_norm, gate_b):
    ob = ob_ref[...]
    if gate_b:
        zb_ref, y_ref = rest
        ob = ob * jax.nn.silu(zb_ref[...])
    else:
        (y_ref,) = rest
    ya = jnp.dot(oa_ref[...].astype(BF16), wpa_ref[...], preferred_element_type=F32)
    yb = jnp.dot(ob.astype(BF16), wpb_ref[...], preferred_element_type=F32)
    merged = jax.nn.sigmoid(ga_ref[...]) * ya + jax.nn.sigmoid(gb_ref[...]) * yb
    y = x_ref[...] + jnp.dot(merged.astype(BF16), wo_ref[...], preferred_element_type=F32)
    if final_norm:
        y = (y * lax.rsqrt(jnp.mean(y * y, axis=-1, keepdims=True) + EPS)) * fg_ref[...]
    y_ref[...] = y


def _merge(x2d, oa, ob, zb, ga, gb, wpa, wpb, wo, final_g, final_norm, tm):
    m, d = x2d.shape
    row = lambda w: pl.BlockSpec((tm, w), lambda i: (i, 0))
    const = lambda i: (0, 0)
    gate_b = zb is not None
    in_specs = [row(d), row(oa.shape[1]), row(ob.shape[1]), row(d), row(d),
                pl.BlockSpec(wpa.shape, const), pl.BlockSpec(wpb.shape, const), pl.BlockSpec(wo.shape, const),
                pl.BlockSpec((1, d), const)]
    args = [x2d, oa, ob, ga, gb, wpa, wpb, wo, final_g.reshape(1, d)]
    if gate_b:
        in_specs.append(row(zb.shape[1]))
        args.append(zb)
    return pl.pallas_call(
        functools.partial(_merge_body, final_norm=final_norm, gate_b=gate_b),
        grid=(m // tm,),
        in_specs=in_specs,
        out_specs=row(d),
        out_shape=jax.ShapeDtypeStruct((m, d), F32),
        compiler_params=_params(("parallel",)),
        name="merge",
    )(*args)


def _pack_in_weights(w_in_l, conv_dim, d_a, nh_a, d_b, nh_b, d_model):
    sizes = (conv_dim, d_a, nh_a, nh_a, 3 * d_b, nh_b, d_b, d_model, d_model)
    offs = [0]
    for s in sizes:
        offs.append(offs[-1] + s)
    col = lambda i: w_in_l[:, offs[i]:offs[i + 1]]
    wbig = jnp.concatenate([col(0), col(1), col(4), col(6), col(7), col(8)], axis=1).astype(BF16)
    small = jnp.concatenate([col(5), col(2), col(3)], axis=1)
    wsmall = jnp.pad(small, ((0, 0), (0, LANES - small.shape[1]))).astype(BF16)
    wsmall_t = small.T.astype(BF16)
    return wbig, wsmall, wsmall_t


def _layer(x2d, nb, t, t_valid, conv0, s0, attend, lw, final_g, final_norm, tm, lb, tc):
    (norm_g, w_in, conv_w, a_log, dt_bias, onorm_g, b_f, w_pa, w_pb, w_o) = lw
    d_model = x2d.shape[1]
    nh_a, dk, dv = s0.shape[1:]
    conv_dim = conv_w.shape[1]
    d_a = nh_a * dv
    nh_b = b_f.shape[0]
    d_b = w_pb.shape[0]
    wbig, wsmall, wsmall_t = _pack_in_weights(w_in, conv_dim, d_a, nh_a, d_b, nh_b, d_model)
    widths = (conv_dim, d_a, d_b, d_b, d_b, d_b, d_model, d_model)
    dtypes = (F32, F32, BF16, F32, F32, F32, F32, F32)
    qkva, za, qb, kb, vb, zb, ga, gb, small, small_t = _inproj(x2d, norm_g, wbig, wsmall, wsmall_t, widths, dtypes, tm)
    logf, ccol, crow = _fgate(small, small_t, b_f, nb, t, tc)
    oa, s_fin = _delta(qkva, small, za, conv0, s0, conv_w, a_log, dt_bias, onorm_g, nb, t, lb, t_valid, nh_b)
    ob, ob_gated = attend(qb, kb, vb, ccol, crow, zb)
    y = _merge(x2d, oa, ob, None if ob_gated else zb, ga, gb, w_pa.astype(BF16), w_pb.astype(BF16),
               w_o.astype(BF16), final_g, final_norm, tm)
    return y, (kb, vb, logf, s_fin, qkva)


def kernel(x_prompt, x_sample, cache_k, cache_v, cache_logf, state_delta, state_conv, page_table, norm_g, w_in,
           conv_w, A_log, dt_bias, onorm_g, b_f, w_pa, w_pb, w_o, final_g):
    depth = norm_g.shape[0]
    nbp, tp_len, d_model = x_prompt.shape
    nbs, ts_len, _ = x_sample.shape
    nh_a, dk, dv = state_delta.shape[2:]
    conv_k, conv_dim = conv_w.shape[1:]
    nh_b, dh_b = cache_k.shape[3:]
    d_b = nh_b * dh_b
    ts_pad = LANES
    assert ts_len <= CHUNK and tp_len % LANES == 0 and conv_k - 1 <= SUBLANES and ts_len >= conv_k - 1

    hp = x_prompt.reshape(nbp * tp_len, d_model)
    hs = jnp.pad(x_sample, ((0, 0), (0, ts_pad - ts_len), (0, 0))).reshape(nbs * ts_pad, d_model)
    conv0_p = jnp.zeros((nbp, SUBLANES, conv_dim), F32)
    s0_p = jnp.zeros((nbp, nh_a, dk, dv), F32)
    n_pages = page_table.shape[1]
    page = cache_k.shape[2]
    assert dk == LANES and dv == LANES and (page * nh_b) % LANES == 0 and LANES % nh_b == 0
    logf_flat = cache_logf.reshape(depth, cache_logf.shape[1], page * nh_b // LANES, LANES)
    pages = 8 if n_pages % 8 == 0 else 1
    tq = 256 if tp_len % 256 == 0 else LANES
    lb_p = 256 if tp_len % 256 == 0 else LANES
    tm_p = 256 if (nbp * tp_len) % 256 == 0 else LANES
    tm_s = 256 if (nbs * ts_pad) % 256 == 0 else LANES
    tc_p = 512 if tp_len % 512 == 0 else LANES

    st_p, st_s = [], []
    for l in range(depth):
        lw = (norm_g[l], w_in[l], conv_w[l], A_log[l], dt_bias[l], onorm_g[l], b_f[l], w_pa[l], w_pb[l], w_o[l])
        last = l == depth - 1

        def attend_p(qb, kb, vb, ccol, crow, zb):
            return _fox_prompt(qb, kb, vb, ccol, crow, zb, nb=nbp, t=tp_len, nh=nh_b, dh=dh_b, tq=tq), True

        hp, sp = _layer(hp, nbp, tp_len, None, conv0_p, s0_p, attend_p, lw, final_g, last, tm_p, lb_p, tc_p)

        conv0_s = jnp.pad(state_conv[l], ((0, 0), (SUBLANES - (conv_k - 1), 0), (0, 0)))

        def attend_s(qb, kb, vb, ccol, crow, zb, l=l):
            del crow, zb
            return _fox_sample(qb, kb, vb, ccol, cache_k, cache_v, logf_flat, page_table, l, nbs, ts_pad,
                               ts_len, pages), False

        hs, ss = _layer(hs, nbs, ts_pad, ts_len, conv0_s, state_delta[l], attend_s, lw, final_g, last,
                        tm_s, ts_pad, ts_pad)
        st_p.append(sp)
        st_s.append(ss)

    def seq_view(a, nb, t, keep):
        return a.reshape(nb, t, *a.shape[1:])[:, :keep]

    y_prompt = hp.reshape(nbp, tp_len, d_model)
    y_sample = seq_view(hs, nbs, ts_pad, ts_len)
    outs_p, outs_s = [], []
    for (st, nb, t, keep, outs) in ((st_p, nbp, tp_len, tp_len, outs_p), (st_s, nbs, ts_pad, ts_len, outs_s)):
        k_all = jnp.stack([seq_view(s[0], nb, t, keep).reshape(nb, keep, nh_b, dh_b) for s in st])
        v_all = jnp.stack([seq_view(s[1], nb, t, keep).reshape(nb, keep, nh_b, dh_b) for s in st])
        logf_all = jnp.stack([s[2][:, :keep] for s in st])
        delta_all = jnp.stack([s[3] for s in st])
        conv_all = jnp.stack([seq_view(s[4], nb, t, keep)[:, keep - (conv_k - 1):] for s in st])
        outs.extend([k_all, v_all, logf_all, delta_all, conv_all])
    return (y_prompt, y_sample, *outs_p, *outs_s)
```

```python
import functools
import math

import jax
import jax.numpy as jnp
from jax import lax
from jax.experimental import pallas as pl
from jax.experimental.pallas import tpu as pltpu

F32 = jnp.float32
BF16 = jnp.bfloat16
EPS = 1e-6
CHUNK = 64
LANES = 128
SUBLANES = 8
SMALL_COLS = 16
VMEM_LIMIT = 56 * 1024 * 1024
FOX_HEADS_PER_PASS = 8
FOX_TK = 256
SAMPLE_PAGES_PER_STEP = 16

_NT = (((1,), (1,)), ((), ()))
_TN = (((0,), (0,)), ((), ()))


def _bdot(a, b):
    return jnp.dot(a.astype(BF16), b.astype(BF16), preferred_element_type=F32)


def _bdot_nt(a, b):
    return lax.dot_general(a.astype(BF16), b.astype(BF16), _NT, preferred_element_type=F32)


def _bdot_tn(a, b):
    return lax.dot_general(a.astype(BF16), b.astype(BF16), _TN, preferred_element_type=F32)


def _split3(x):
    x1 = x.astype(BF16)
    r1 = x - x1.astype(F32)
    x2 = r1.astype(BF16)
    x3 = (r1 - x2.astype(F32)).astype(BF16)
    return x1, x2, x3


def _xdot(mask_bf16, x, dims=None):
    parts = _split3(x)
    if dims is None:
        return sum(jnp.dot(mask_bf16, p, preferred_element_type=F32) for p in parts)
    return sum(lax.dot_general(mask_bf16, p, dims, preferred_element_type=F32) for p in parts)


def _xdot_r(x, mask_bf16, dims=None):
    parts = _split3(x)
    if dims is None:
        return sum(jnp.dot(p, mask_bf16, preferred_element_type=F32) for p in parts)
    return sum(lax.dot_general(p, mask_bf16, dims, preferred_element_type=F32) for p in parts)


def _iota2(shape, dim):
    return lax.broadcasted_iota(jnp.int32, shape, dim)


def _params(sem):
    return pltpu.CompilerParams(dimension_semantics=sem, vmem_limit_bytes=VMEM_LIMIT)


def _inproj_body(x_ref, g_ref, w_ref, ws_ref, wst_ref, wkvt_ref, *out_refs, segs):
    x = x_ref[...]
    h = (x * lax.rsqrt(jnp.mean(x * x, axis=-1, keepdims=True) + EPS)) * g_ref[...]
    hb = h.astype(BF16)
    for (lo, hi), o_ref in zip(segs, out_refs[:-4]):
        o_ref[...] = jnp.dot(hb, w_ref[:, lo:hi], preferred_element_type=F32).astype(o_ref.dtype)
    sm_ref, smt_ref, kt_ref, vt_ref = out_refs[-4:]
    sm_ref[...] = jnp.dot(hb, ws_ref[...], preferred_element_type=F32)
    smt_ref[...] = lax.dot_general(wst_ref[...], hb, _NT, preferred_element_type=F32)
    d_b = kt_ref.shape[0]
    kvt = lax.dot_general(wkvt_ref[...], hb, _NT, preferred_element_type=F32)
    kt_ref[...] = kvt[:d_b]
    vt_ref[...] = kvt[d_b:]


def _inproj(x2d, g, wbig, wsmall, wsmall_t, wkv_t, widths, dtypes, tm, nb):
    m, d = x2d.shape
    t = m // nb
    nj = t // tm
    d_b = wkv_t.shape[0] // 2
    offs = [0]
    for w in widths:
        offs.append(offs[-1] + w)
    segs = tuple((offs[i], offs[i + 1]) for i in range(len(widths)))
    const = lambda i: (0, 0)
    seq_t = pl.BlockSpec((None, d_b, tm), lambda i: (i // nj, 0, i % nj))
    out_shape = [jax.ShapeDtypeStruct((m, w), dt) for w, dt in zip(widths, dtypes)]
    out_shape += [jax.ShapeDtypeStruct((m, LANES), F32), jax.ShapeDtypeStruct((SMALL_COLS, m), F32),
                  jax.ShapeDtypeStruct((nb, d_b, t), F32), jax.ShapeDtypeStruct((nb, d_b, t), F32)]
    out_specs = [pl.BlockSpec((tm, w), lambda i: (i, 0)) for w in widths]
    out_specs += [pl.BlockSpec((tm, LANES), lambda i: (i, 0)), pl.BlockSpec((SMALL_COLS, tm), lambda i: (0, i)),
                  seq_t, seq_t]
    return pl.pallas_call(
        functools.partial(_inproj_body, segs=segs),
        grid=(m // tm,),
        in_specs=[pl.BlockSpec((tm, d), lambda i: (i, 0)),
                  pl.BlockSpec((1, d), const),
                  pl.BlockSpec(wbig.shape, const, pipeline_mode=pl.Buffered(1)),
                  pl.BlockSpec(wsmall.shape, const, pipeline_mode=pl.Buffered(1)),
                  pl.BlockSpec(wsmall_t.shape, const, pipeline_mode=pl.Buffered(1)),
                  pl.BlockSpec(wkv_t.shape, const, pipeline_mode=pl.Buffered(1))],
        out_specs=out_specs,
        out_shape=out_shape,
        compiler_params=_params(("parallel",)),
        name="inproj",
    )(x2d, g.reshape(1, d), wbig, wsmall, wsmall_t, wkv_t)


def _fgate_body(sm_ref, smt_ref, bfr_ref, bfc_ref, logf_ref, ccol_ref, crow_ref, carry_c, carry_r, *, nh):
    j = pl.program_id(1)
    tc = sm_ref.shape[0]

    @pl.when(j == 0)
    def _():
        carry_c[...] = jnp.zeros_like(carry_c)
        carry_r[...] = jnp.zeros_like(carry_r)

    logf_c = jax.nn.log_sigmoid(sm_ref[...] + bfr_ref[...])
    tril = (_iota2((tc, tc), 0) >= _iota2((tc, tc), 1)).astype(BF16)
    ccol = _xdot(tril, logf_c) + carry_c[...]
    ccol_ref[...] = ccol[:, :nh]
    carry_c[...] = ccol[tc - 1:tc, :]

    logf_r = jax.nn.log_sigmoid(smt_ref[0:nh, :] + bfc_ref[...])
    logf_ref[...] = logf_r
    triu = (_iota2((tc, tc), 0) <= _iota2((tc, tc), 1)).astype(BF16)
    crow = _xdot_r(logf_r, triu) + carry_r[:, 0:1]
    crow_ref[...] = crow
    carry_r[...] = jnp.broadcast_to(crow[:, tc - 1:tc], carry_r.shape)


def _fgate(small, small_t, b_f, nb, t, tc):
    nh = b_f.shape[0]
    nj = t // tc
    bfr = jnp.zeros((1, LANES), F32).at[0, :nh].set(b_f)
    bfc = b_f.reshape(nh, 1)
    return pl.pallas_call(
        functools.partial(_fgate_body, nh=nh),
        grid=(nb, nj),
        in_specs=[pl.BlockSpec((tc, LANES), lambda b, j: (b * nj + j, 0)),
                  pl.BlockSpec((SMALL_COLS, tc), lambda b, j: (0, b * nj + j)),
                  pl.BlockSpec((1, LANES), lambda b, j: (0, 0)),
                  pl.BlockSpec((nh, 1), lambda b, j: (0, 0))],
        out_specs=[pl.BlockSpec((None, nh, tc), lambda b, j: (b, 0, j)),
                   pl.BlockSpec((None, tc, nh), lambda b, j: (b, j, 0)),
                   pl.BlockSpec((None, nh, tc), lambda b, j: (b, 0, j))],
        out_shape=[jax.ShapeDtypeStruct((nb, nh, t), F32),
                   jax.ShapeDtypeStruct((nb, t, nh), F32),
                   jax.ShapeDtypeStruct((nb, nh, t), F32)],
        scratch_shapes=[pltpu.VMEM((1, LANES), F32), pltpu.VMEM((nh, LANES), F32)],
        compiler_params=_params(("parallel", "arbitrary")),
        name="fgate",
    )(small, small_t, bfr, bfc)


def _delta_body(qkv_ref, sm_ref, za_ref, conv0_ref, s0_ref, cw_ref, alog_ref, dtb_ref, og_ref,
                oa_ref, sfin_ref, xbuf, s_scr, *, nh, dk, dv, conv_k, t_valid, n_chunks, beta_lane):
    j = pl.program_id(1)
    lb = qkv_ref.shape[0]
    c = CHUNK
    pad = SUBLANES

    @pl.when(j == 0)
    def _():
        xbuf[0:pad, :] = conv0_ref[...]
        s_scr[...] = s0_ref[...]

    @pl.when(j > 0)
    def _():
        xbuf[0:pad, :] = xbuf[lb:lb + pad, :]

    xbuf[pad:pad + lb, :] = qkv_ref[...]

    acc = None
    for tap in range(conv_k):
        start = pad - (conv_k - 1) + tap
        term = cw_ref[tap:tap + 1, :] * xbuf[start:start + lb, :]
        acc = term if acc is None else acc + term
    y = jax.nn.silu(acc)

    sm = sm_ref[...]
    beta_all = jax.nn.sigmoid(sm)
    g_all = -jnp.exp(alog_ref[...]) * jax.nn.softplus(sm + dtb_ref[...])
    if t_valid is not None:
        tok = j * lb + _iota2((lb, LANES), 0)
        valid = tok < t_valid
        beta_all = jnp.where(valid, beta_all, 0.0)
        g_all = jnp.where(valid, g_all, 0.0)

    ri = _iota2((c, c), 0)
    ci = _iota2((c, c), 1)
    tril_b = (ri >= ci).astype(BF16)
    lower = ri >= ci
    strict = ri > ci
    eye = (ri == ci).astype(F32)
    og = og_ref[...]
    qscale = dk ** -0.5
    kbase = nh * dk
    vbase = 2 * nh * dk
    a_lane = beta_lane + nh

    probs = [(ch, h) for ch in range(n_chunks) for h in range(nh)]
    strict_w = _iota2((c, LANES), 0) > _iota2((c, LANES), 1)
    qs, ks, vs, betas = {}, {}, {}, {}
    diffs, gcums = {}, {}
    for ch in range(n_chunks):
        r0 = ch * c
        cols = []
        for h in range(nh):
            q = y[r0:r0 + c, h * dk:(h + 1) * dk]
            k = y[r0:r0 + c, kbase + h * dk:kbase + (h + 1) * dk]
            qs[ch, h] = q * lax.rsqrt(jnp.sum(q * q, axis=-1, keepdims=True) + EPS) * qscale
            ks[ch, h] = k * lax.rsqrt(jnp.sum(k * k, axis=-1, keepdims=True) + EPS)
            vs[ch, h] = y[r0:r0 + c, vbase + h * dv:vbase + (h + 1) * dv]
            betas[ch, h] = beta_all[r0:r0 + c, beta_lane + h:beta_lane + h + 1]
            g = g_all[r0:r0 + c, a_lane + h:a_lane + h + 1]
            g_w = jnp.broadcast_to(g, (c, LANES))
            cols += [jnp.where(strict_w, g_w, 0.0), g_w]
        cum = _xdot(tril_b, jnp.concatenate(cols, axis=1))
        for h in range(nh):
            diffs[ch, h] = cum[:, 2 * h * LANES:2 * h * LANES + c]
            gcums[ch, h] = cum[:, (2 * h + 1) * LANES:(2 * h + 2) * LANES]

    decays, egcs, kbs, n_pows, t_invs = {}, {}, {}, {}, {}
    for p in probs:
        decays[p] = jnp.where(lower, jnp.exp(diffs[p]), 0.0)
        egcs[p] = jnp.exp(gcums[p])
        kbs[p] = ks[p] * betas[p]
    for p in probs:
        n_pows[p] = -jnp.where(strict, _bdot_nt(kbs[p], ks[p]) * decays[p], 0.0)
        t_invs[p] = eye + n_pows[p]
    span = 2
    while span < c:
        for p in probs:
            n_pows[p] = _bdot(n_pows[p], n_pows[p])
        for p in probs:
            t_invs[p] = t_invs[p] + _bdot(t_invs[p], n_pows[p])
        span *= 2
    us, ws, attns = {}, {}, {}
    for p in probs:
        uw = _bdot(t_invs[p], jnp.concatenate([vs[p] * betas[p], kbs[p] * egcs[p]], axis=1))
        us[p] = uw[:, :dv]
        ws[p] = uw[:, dv:]
    for p in probs:
        attns[p] = _bdot_nt(qs[p], ks[p]) * decays[p]

    for ch in range(n_chunks):
        r0 = ch * c
        s_olds = [s_scr[h] for h in range(nh)]
        wss = [_bdot(jnp.concatenate([ws[ch, h], qs[ch, h] * egcs[ch, h]], axis=0), s_olds[h]) for h in range(nh)]
        v_news = [us[ch, h] - wss[h][:c] for h in range(nh)]
        os_ = [wss[h][c:] + _bdot(attns[ch, h], v_news[h]) for h in range(nh)]
        for h in range(nh):
            gcum = gcums[ch, h]
            glast = gcum[c - 1:c, :]
            kd = ks[ch, h] * jnp.exp(glast - gcum)
            s_scr[h] = s_olds[h] * jnp.exp(glast[:, :dv]) + _bdot_tn(kd, v_news[h])
        for h in range(nh):
            o = os_[h]
            o = o * lax.rsqrt(jnp.mean(o * o, axis=-1, keepdims=True) + EPS) * og
            z = za_ref[r0:r0 + c, h * dv:(h + 1) * dv]
            oa_ref[r0:r0 + c, h * dv:(h + 1) * dv] = (o * jax.nn.silu(z)).astype(oa_ref.dtype)

    @pl.when(j == pl.num_programs(1) - 1)
    def _():
        sfin_ref[...] = s_scr[...]


def _delta(qkva, small, za, conv0, s0, conv_w, a_log, dt_bias, onorm_g, nb, t, lb, t_valid, beta_lane):
    nh, dk, dv = s0.shape[1:]
    conv_k, conv_dim = conv_w.shape
    nj = t // lb
    a_lane = beta_lane + nh
    alog_row = jnp.zeros((1, LANES), F32).at[0, a_lane:a_lane + nh].set(a_log)
    dtb_row = jnp.zeros((1, LANES), F32).at[0, a_lane:a_lane + nh].set(dt_bias)
    const = lambda b, j: (0, 0)
    body = functools.partial(_delta_body, nh=nh, dk=dk, dv=dv, conv_k=conv_k, t_valid=t_valid,
                             n_chunks=lb // CHUNK, beta_lane=beta_lane)
    return pl.pallas_call(
        body,
        grid=(nb, nj),
        in_specs=[pl.BlockSpec((lb, conv_dim), lambda b, j: (b * nj + j, 0)),
                  pl.BlockSpec((lb, LANES), lambda b, j: (b * nj + j, 0)),
                  pl.BlockSpec((lb, nh * dv), lambda b, j: (b * nj + j, 0)),
                  pl.BlockSpec((None, SUBLANES, conv_dim), lambda b, j: (b, 0, 0)),
                  pl.BlockSpec((None, nh, dk, dv), lambda b, j: (b, 0, 0, 0)),
                  pl.BlockSpec((conv_k, conv_dim), const),
                  pl.BlockSpec((1, LANES), const),
                  pl.BlockSpec((1, LANES), const),
                  pl.BlockSpec((1, dv), const)],
        out_specs=[pl.BlockSpec((lb, nh * dv), lambda b, j: (b * nj + j, 0)),
                   pl.BlockSpec((None, nh, dk, dv), lambda b, j: (b, 0, 0, 0))],
        out_shape=[jax.ShapeDtypeStruct((nb * t, nh * dv), BF16),
                   jax.ShapeDtypeStruct((nb, nh, dk, dv), F32)],
        scratch_shapes=[pltpu.VMEM((lb + 2 * SUBLANES, conv_dim), F32),
                        pltpu.VMEM((nh, dk, dv), F32)],
        compiler_params=_params(("parallel", "arbitrary")),
        name="delta",
    )(qkva, small, za, conv0, s0, conv_w, alog_row, dtb_row, onorm_g.reshape(1, dv))


def _is_pow2(x):
    return math.frexp(x)[0] == 0.5


def _fox_body(q_ref, k_ref, v_ref, ccol_ref, crow_ref, zb_ref, ob_ref, *, nh, dh, tk, heads_per_pass):
    qi = pl.program_id(1)
    tq = q_ref.shape[0]
    scale = dh ** -0.5
    pre_scale = _is_pow2(scale)
    lane = _iota2((1, LANES), 1)
    frow = _iota2((LANES, 1), 0)
    per = LANES // dh
    krow = _iota2((tk, tq), 0)
    qcol = _iota2((tk, tq), 1)
    qs0 = pl.multiple_of(qi * tq, tq)

    for h0 in range(0, nh, heads_per_pass):
        heads = list(range(h0, h0 + heads_per_pass))
        gls, halves, qhs, cts = [], [], [], []
        for h in heads:
            grp, hh = divmod(h, per)
            gl = slice(grp * LANES, (grp + 1) * LANES)
            in_head = (lane >= hh * dh) & (lane < (hh + 1) * dh)
            q2 = q_ref[:, gl]
            if pre_scale:
                q2 = q2 * jnp.asarray(scale, q2.dtype)
            gls.append(gl)
            halves.append(hh)
            qhs.append(jnp.where(in_head, q2, jnp.zeros_like(q2)))
            cts.append(crow_ref[h:h + 1, pl.ds(qs0, tq)])

        def step(jb, carry, diag=None):
            masked = diag is not None
            ks = pl.multiple_of(jb * tk, tk)
            ss = []
            for i, h in enumerate(heads):
                s = _bdot_nt(k_ref[pl.ds(ks, tk), gls[i]], qhs[i])
                if not pre_scale:
                    s = s * scale
                ss.append(s)
            ps, new = [], []
            for i, h in enumerate(heads):
                m, l, acc = carry[i]
                cs = ccol_ref[pl.ds(ks, tk), h:h + 1]
                s = ss[i] + cts[i] - cs
                if masked:
                    s = jnp.where(krow + diag * tk <= qcol, s, -jnp.inf)
                m_new = jnp.maximum(m, jnp.max(s, axis=0, keepdims=True))
                corr = jnp.exp(m - m_new)
                p = jnp.exp(s - m_new)
                l = l * corr + jnp.sum(p, axis=0, keepdims=True)
                ps.append(p.astype(BF16))
                new.append((m_new, l, acc * corr))
            out = []
            for i, h in enumerate(heads):
                m_new, l, acc = new[i]
                pv = jnp.dot(v_ref[gls[i], pl.ds(ks, tk)].astype(BF16), ps[i], preferred_element_type=F32)
                out.append((m_new, l, acc + pv))
            return tuple(out)

        init = tuple((jnp.full((1, tq), -jnp.inf, F32), jnp.zeros((1, tq), F32), jnp.zeros((LANES, tq), F32))
                     for _ in heads)
        per_q = tq // tk
        fin = lax.fori_loop(0, qi * per_q, step, init)
        for d in range(per_q):
            fin = step(qi * per_q + d, fin, diag=d)
        for g0 in range(0, heads_per_pass, per):
            o2 = None
            for i in range(g0, g0 + per):
                _, l, acc = fin[i]
                o = acc / l
                keep = (frow >= halves[i] * dh) & (frow < (halves[i] + 1) * dh)
                o2 = o if o2 is None else jnp.where(keep, o, o2)
            gl = gls[g0]
            ob_ref[:, gl] = (o2.T * jax.nn.silu(zb_ref[:, gl])).astype(ob_ref.dtype)


def _fox_prompt(qb, kb, v3, ccol, crow, zb, nb, t, nh, dh, tq):
    d_b = nh * dh
    nq = t // tq
    q3 = qb.reshape(nb, t, d_b)
    k3 = kb.reshape(nb, t, d_b)
    z3 = zb.reshape(nb, t, d_b)
    out = pl.pallas_call(
        functools.partial(_fox_body, nh=nh, dh=dh, tk=min(FOX_TK, tq), heads_per_pass=FOX_HEADS_PER_PASS),
        grid=(nb, nq),
        in_specs=[pl.BlockSpec((None, tq, d_b), lambda b, i: (b, i, 0)),
                  pl.BlockSpec((None, t, d_b), lambda b, i: (b, 0, 0)),
                  pl.BlockSpec((None, d_b, t), lambda b, i: (b, 0, 0)),
                  pl.BlockSpec((None, t, nh), lambda b, i: (b, 0, 0)),
                  pl.BlockSpec((None, nh, t), lambda b, i: (b, 0, 0)),
                  pl.BlockSpec((None, tq, d_b), lambda b, i: (b, i, 0))],
        out_specs=pl.BlockSpec((None, tq, d_b), lambda b, i: (b, i, 0)),
        out_shape=jax.ShapeDtypeStruct((nb, t, d_b), BF16),
        compiler_params=_params(("parallel", "arbitrary")),
        name="fox_prompt",
    )(q3, k3, v3, ccol, crow, z3)
    return out.reshape(nb * t, d_b)


def _fox_sample_body(pt_ref, q_ref, kn_ref, vn_ref, crow_ref, *refs, nh, dh, t_new, pages):
    del pt_ref
    k_refs = refs[0:pages]
    v_refs = refs[pages:2 * pages]
    lf_refs = refs[2 * pages:3 * pages]
    ob_ref = refs[3 * pages]
    qblk, m_scr, l_scr, acc_scr, carry_scr = refs[3 * pages + 1:]
    j = pl.program_id(1)
    d_b = nh * dh
    nrow = t_new * nh
    page = k_refs[0].shape[1]
    tp = crow_ref.shape[1]
    scale = dh ** -0.5
    head_mask = (_iota2((nh, d_b), 1) // dh) == _iota2((nh, d_b), 0)
    c_tiled = jnp.concatenate([crow_ref[...]] * t_new, axis=0)
    row_tok = _iota2((nrow, tp), 0) // nh
    lane_tok = _iota2((nrow, tp), 1)
    c_own = jnp.sum(jnp.where(lane_tok == row_tok, c_tiled, 0.0), axis=-1, keepdims=True)

    @pl.when(j == 0)
    def _():
        for t in range(t_new):
            qrow = jnp.broadcast_to(q_ref[t:t + 1, :].astype(F32), (nh, d_b))
            qblk[t * nh:(t + 1) * nh, :] = jnp.where(head_mask, qrow, 0.0)
        m_scr[...] = jnp.full(m_scr.shape, -jnp.inf, F32)
        l_scr[...] = jnp.zeros_like(l_scr)
        acc_scr[...] = jnp.zeros_like(acc_scr)
        carry_scr[...] = jnp.zeros_like(carry_scr)

    def update(s_tiles, v_blocks):
        tmax = [jnp.max(s, axis=-1, keepdims=True) for s in s_tiles]
        m_old = m_scr[...]
        m_new = functools.reduce(jnp.maximum, tmax, m_old)
        corr = jnp.exp(m_old - m_new)
        p_tiles = [jnp.exp(s - m_new) for s in s_tiles]
        psum = functools.reduce(lambda a, b: a + b, [jnp.sum(p, axis=-1, keepdims=True) for p in p_tiles])
        l_scr[...] = l_scr[...] * corr + psum
        pv = functools.reduce(lambda a, b: a + b, [_bdot_nt(p, v) for p, v in zip(p_tiles, v_blocks)])
        acc_scr[...] = acc_scr[...] * corr + pv
        m_scr[...] = m_new

    after = (_iota2((page, page), 0) > _iota2((page, page), 1)).astype(BF16)
    sums = jnp.concatenate([after, jnp.ones((page, page), BF16)], axis=1)
    lf = jnp.concatenate([lf_refs[g][...] for g in range(pages)], axis=0)
    cum = _xdot_r(lf, sums)
    run = carry_scr[...]
    biases = []
    for g in range(pages):
        biases.append(cum[g * nh:(g + 1) * nh, :page] + run)
        run = run + cum[g * nh:(g + 1) * nh, page:]
    carry_scr[...] = run

    qb = qblk[...].astype(BF16)
    raw = [jnp.dot(qb, k_refs[g][...].astype(BF16), preferred_element_type=F32) for g in range(pages)]
    s_tiles = [raw[g] * scale + jnp.concatenate([biases[g]] * t_new, axis=0) + c_own for g in range(pages)]
    update(s_tiles, [v_refs[g][...] for g in range(pages)])

    @pl.when(j == pl.num_programs(1) - 1)
    def _():
        s = jnp.dot(qb, kn_ref[...].astype(BF16), preferred_element_type=F32) * scale + c_own - c_tiled
        s = jnp.where(lane_tok <= row_tok, s, -jnp.inf)
        update([s], [vn_ref[...]])
        o = acc_scr[...] / l_scr[...]
        ob_ref[...] = jnp.zeros_like(ob_ref)
        for t in range(t_new):
            blk = jnp.where(head_mask, o[t * nh:(t + 1) * nh, :], 0.0)
            ob_ref[t:t + 1, :] = jnp.sum(blk, axis=0, keepdims=True)


def _fox_sample(qb, k3, v3, crow, cache_kt, cache_vt, cache_lft, page_table, layer, nb, tp, t_new, pages):
    d_b, page = cache_kt.shape[2:]
    nh = cache_lft.shape[2]
    dh = d_b // nh
    n_pages = page_table.shape[1]
    nj = n_pages // pages
    q3 = qb.reshape(nb, tp, d_b)

    def page_map(g):
        return lambda b, j, pt: (layer, pt[b, n_pages - 1 - (j * pages + g)], 0, 0)

    seq = lambda b, j, pt: (b, 0, 0)
    in_specs = [pl.BlockSpec((None, tp, d_b), seq),
                pl.BlockSpec((None, d_b, tp), seq),
                pl.BlockSpec((None, d_b, tp), seq),
                pl.BlockSpec((None, nh, tp), seq)]
    in_specs += [pl.BlockSpec((None, None, d_b, page), page_map(g)) for g in range(pages)]
    in_specs += [pl.BlockSpec((None, None, d_b, page), page_map(g)) for g in range(pages)]
    in_specs += [pl.BlockSpec((None, None, nh, page), page_map(g)) for g in range(pages)]
    nrow = t_new * nh
    grid_spec = pltpu.PrefetchScalarGridSpec(
        num_scalar_prefetch=1,
        grid=(nb, nj),
        in_specs=in_specs,
        out_specs=pl.BlockSpec((None, tp, d_b), seq),
        scratch_shapes=[pltpu.VMEM((nrow, d_b), F32),
                        pltpu.VMEM((nrow, 1), F32),
                        pltpu.VMEM((nrow, 1), F32),
                        pltpu.VMEM((nrow, d_b), F32),
                        pltpu.VMEM((nh, page), F32)],
    )
    out = pl.pallas_call(
        functools.partial(_fox_sample_body, nh=nh, dh=dh, t_new=t_new, pages=pages),
        grid_spec=grid_spec,
        out_shape=jax.ShapeDtypeStruct((nb, tp, d_b), F32),
        compiler_params=_params(("parallel", "arbitrary")),
        name="fox_sample",
    )(page_table, q3, k3, v3, crow, *([cache_kt] * pages), *([cache_vt] * pages), *([cache_lft] * pages))
    return out.reshape(nb * tp, d_b)


def _merge_body(x_ref, oa_ref, ob_ref, ga_ref, gb_ref, wpa_ref, wpb_ref, wo_ref, fg_ref, *rest, final_norm, gate_b):
    ob = ob_ref[...]
    if gate_b:
        zb_ref, y_ref = rest
        ob = ob * jax.nn.silu(zb_ref[...])
    else:
        (y_ref,) = rest
    ya = jnp.dot(oa_ref[...].astype(BF16), wpa_ref[...], preferred_element_type=F32)
    yb = jnp.dot(ob.astype(BF16), wpb_ref[...], preferred_element_type=F32)
    merged = jax.nn.sigmoid(ga_ref[...]) * ya + jax.nn.sigmoid(gb_ref[...]) * yb
    y = x_ref[...] + jnp.dot(merged.astype(BF16), wo_ref[...], preferred_element_type=F32)
    if final_norm:
        y = (y * lax.rsqrt(jnp.mean(y * y, axis=-1, keepdims=True) + EPS)) * fg_ref[...]
    y_ref[...] = y


def _merge(x2d, oa, ob, zb, ga, gb, wpa, wpb, wo, final_g, final_norm, tm):
    m, d = x2d.shape
    row = lambda w: pl.BlockSpec((tm, w), lambda i: (i, 0))
    const = lambda i: (0, 0)
    gate_b = zb is not None
    in_specs = [row(d), row(oa.shape[1]), row(ob.shape[1]), row(d), row(d),
                pl.BlockSpec(wpa.shape, const), pl.BlockSpec(wpb.shape, const), pl.BlockSpec(wo.shape, const),
                pl.BlockSpec((1, d), const)]
    args = [x2d, oa, ob, ga, gb, wpa, wpb, wo, final_g.reshape(1, d)]
    if gate_b:
        in_specs.append(row(zb.shape[1]))
        args.append(zb)
    return pl.pallas_call(
        functools.partial(_merge_body, final_norm=final_norm, gate_b=gate_b),
        grid=(m // tm,),
        in_specs=in_specs,
        out_specs=row(d),
        out_shape=jax.ShapeDtypeStruct((m, d), F32),
        compiler_params=_params(("parallel",)),
        name="merge",
    )(*args)


def _pack_in_weights(w_in_l, conv_dim, d_a, nh_a, d_b, nh_b, d_model):
    sizes = (conv_dim, d_a, nh_a, nh_a, 3 * d_b, nh_b, d_b, d_model, d_model)
    offs = [0]
    for s in sizes:
        offs.append(offs[-1] + s)
    col = lambda i: w_in_l[:, offs[i]:offs[i + 1]]
    qkv_b = col(4)
    wbig = jnp.concatenate([col(0), col(1), qkv_b[:, :2 * d_b], col(6), col(7), col(8)], axis=1).astype(BF16)
    small = jnp.concatenate([col(5), col(2), col(3)], axis=1)
    wsmall = jnp.pad(small, ((0, 0), (0, LANES - small.shape[1]))).astype(BF16)
    wsmall_t = small.T.astype(BF16)
    wkv_t = qkv_b[:, d_b:].T.astype(BF16)
    return wbig, wsmall, wsmall_t, wkv_t


def _layer(x2d, nb, t, t_valid, conv0, s0, attend, lw, final_g, final_norm, tm, lb, tc):
    (norm_g, w_in, conv_w, a_log, dt_bias, onorm_g, b_f, w_pa, w_pb, w_o) = lw
    d_model = x2d.shape[1]
    nh_a, dk, dv = s0.shape[1:]
    conv_dim = conv_w.shape[1]
    d_a = nh_a * dv
    nh_b = b_f.shape[0]
    d_b = w_pb.shape[0]
    wbig, wsmall, wsmall_t, wkv_t = _pack_in_weights(w_in, conv_dim, d_a, nh_a, d_b, nh_b, d_model)
    widths = (conv_dim, d_a, d_b, d_b, d_b, d_model, d_model)
    dtypes = (F32, F32, BF16, BF16, F32, F32, F32)
    qkva, za, qb, kb, zb, ga, gb, small, small_t, kt, vt = _inproj(x2d, norm_g, wbig, wsmall, wsmall_t, wkv_t,
                                                                   widths, dtypes, tm, nb)
    logf_t, ccol, crow = _fgate(small, small_t, b_f, nb, t, tc)
    oa, s_fin = _delta(qkva, small, za, conv0, s0, conv_w, a_log, dt_bias, onorm_g, nb, t, lb, t_valid, nh_b)
    ob, ob_gated = attend(qb, kb, kt, vt, ccol, crow, zb)
    y = _merge(x2d, oa, ob, None if ob_gated else zb, ga, gb, w_pa.astype(BF16), w_pb.astype(BF16),
               w_o.astype(BF16), final_g, final_norm, tm)
    return y, (kt, vt, logf_t, s_fin, qkva)


def kernel(x_prompt, x_sample, cache_k, cache_v, cache_logf, state_delta, state_conv, page_table, norm_g, w_in,
           conv_w, A_log, dt_bias, onorm_g, b_f, w_pa, w_pb, w_o, final_g):
    depth = norm_g.shape[0]
    nbp, tp_len, d_model = x_prompt.shape
    nbs, ts_len, _ = x_sample.shape
    nh_a, dk, dv = state_delta.shape[2:]
    conv_k, conv_dim = conv_w.shape[1:]
    nh_b, dh_b = cache_k.shape[3:]
    d_b = nh_b * dh_b
    ts_pad = LANES
    assert ts_len <= CHUNK and tp_len % LANES == 0 and conv_k - 1 <= SUBLANES and ts_len >= conv_k - 1

    hp = x_prompt.reshape(nbp * tp_len, d_model)
    hs = jnp.pad(x_sample, ((0, 0), (0, ts_pad - ts_len), (0, 0))).reshape(nbs * ts_pad, d_model)
    conv0_p = jnp.zeros((nbp, SUBLANES, conv_dim), F32)
    s0_p = jnp.zeros((nbp, nh_a, dk, dv), F32)
    n_pages = page_table.shape[1]
    page = cache_k.shape[2]
    assert dk == LANES and dv == LANES
    cache_kt = jnp.transpose(cache_k, (0, 1, 3, 4, 2)).reshape(depth, cache_k.shape[1], d_b, page)
    cache_vt = jnp.transpose(cache_v, (0, 1, 3, 4, 2)).reshape(depth, cache_v.shape[1], d_b, page)
    cache_lft = jnp.transpose(cache_logf, (0, 1, 3, 2))
    pages = next(g for g in (SAMPLE_PAGES_PER_STEP, 8, 4, 2, 1) if n_pages % g == 0)
    tq = 256 if tp_len % 256 == 0 else LANES
    lb_p = 256 if tp_len % 256 == 0 else LANES
    tm_p = 256 if tp_len % 256 == 0 else LANES
    tm_s = ts_pad
    tc_p = 512 if tp_len % 512 == 0 else LANES

    st_p, st_s = [], []
    for l in range(depth):
        lw = (norm_g[l], w_in[l], conv_w[l], A_log[l], dt_bias[l], onorm_g[l], b_f[l], w_pa[l], w_pb[l], w_o[l])
        last = l == depth - 1

        def attend_p(qb, kb, kt, vt, ccol, crow, zb):
            del kt
            return _fox_prompt(qb, kb, vt, ccol, crow, zb, nb=nbp, t=tp_len, nh=nh_b, dh=dh_b, tq=tq), True

        hp, sp = _layer(hp, nbp, tp_len, None, conv0_p, s0_p, attend_p, lw, final_g, last, tm_p, lb_p, tc_p)

        conv0_s = jnp.pad(state_conv[l], ((0, 0), (SUBLANES - (conv_k - 1), 0), (0, 0)))

        def attend_s(qb, kb, kt, vt, ccol, crow, zb, l=l):
            del kb, ccol, zb
            return _fox_sample(qb, kt, vt, crow, cache_kt, cache_vt, cache_lft, page_table, l, nbs, ts_pad,
                               ts_len, pages), False

        hs, ss = _layer(hs, nbs, ts_pad, ts_len, conv0_s, state_delta[l], attend_s, lw, final_g, last,
                        tm_s, ts_pad, ts_pad)
        st_p.append(sp)
        st_s.append(ss)

    def seq_view(a, nb, t, keep):
        return a.reshape(nb, t, *a.shape[1:])[:, :keep]

    def token_major(feat_major, nb, t, keep):
        a = feat_major.reshape(depth, nb, nh_b, dh_b, t)[..., :keep]
        return jnp.transpose(a, (0, 1, 4, 2, 3))

    y_prompt = hp.reshape(nbp, tp_len, d_model)
    y_sample = seq_view(hs, nbs, ts_pad, ts_len)
    outs_p, outs_s = [], []
    for (st, nb, t, keep, outs) in ((st_p, nbp, tp_len, tp_len, outs_p), (st_s, nbs, ts_pad, ts_len, outs_s)):
        k_all = token_major(jnp.stack([s[0] for s in st]), nb, t, keep)
        v_all = token_major(jnp.stack([s[1] for s in st]), nb, t, keep)
        logf_all = jnp.transpose(jnp.stack([s[2] for s in st])[..., :keep], (0, 1, 3, 2))
        delta_all = jnp.stack([s[3] for s in st])
        conv_all = jnp.stack([seq_view(s[4], nb, t, keep)[:, keep - (conv_k - 1):] for s in st])
        outs.extend([k_all, v_all, logf_all, delta_all, conv_all])
    return (y_prompt, y_sample, *outs_p, *outs_s)
```

```python
import functools
import math

import jax
import jax.numpy as jnp
from jax import lax
from jax.experimental import pallas as pl
from jax.experimental.pallas import tpu as pltpu

F32 = jnp.float32
BF16 = jnp.bfloat16
EPS = 1e-6
CHUNK = 64
LANES = 128
SUBLANES = 8
BF16_SUBLANES = 16
SMALL_COLS = 16
VMEM_LIMIT = 56 * 1024 * 1024
FOX_HEADS_PER_PASS = 8
FOX_TK = 256
SAMPLE_PAGES_PER_STEP = 16
CONV_COLS = 256
BIAS_LANES = 6

_NT = (((1,), (1,)), ((), ()))
_TN = (((0,), (0,)), ((), ()))


def _bdot(a, b):
    return jnp.dot(a.astype(BF16), b.astype(BF16), preferred_element_type=F32)


def _bdot_nt(a, b):
    return lax.dot_general(a.astype(BF16), b.astype(BF16), _NT, preferred_element_type=F32)


def _bdot_tn(a, b):
    return lax.dot_general(a.astype(BF16), b.astype(BF16), _TN, preferred_element_type=F32)


def _split3(x):
    x1 = x.astype(BF16)
    r1 = x - x1.astype(F32)
    x2 = r1.astype(BF16)
    x3 = (r1 - x2.astype(F32)).astype(BF16)
    return x1, x2, x3


def _xdot(mask_bf16, x, dims=None):
    parts = _split3(x)
    if dims is None:
        return sum(jnp.dot(mask_bf16, p, preferred_element_type=F32) for p in parts)
    return sum(lax.dot_general(mask_bf16, p, dims, preferred_element_type=F32) for p in parts)


def _xdot_r(x, mask_bf16, dims=None):
    parts = _split3(x)
    if dims is None:
        return sum(jnp.dot(p, mask_bf16, preferred_element_type=F32) for p in parts)
    return sum(lax.dot_general(p, mask_bf16, dims, preferred_element_type=F32) for p in parts)


def _iota2(shape, dim):
    return lax.broadcasted_iota(jnp.int32, shape, dim)


def _params(sem):
    return pltpu.CompilerParams(dimension_semantics=sem, vmem_limit_bytes=VMEM_LIMIT)


def _inproj_body(x_ref, g_ref, w_ref, ws_ref, wst_ref, wkvt_ref, conv0_ref, cw_ref, *refs, segs, nj, t_last):
    out_refs, xbuf = refs[:-1], refs[-1]
    i = pl.program_id(0)
    tm = x_ref.shape[0]
    pad = SUBLANES
    conv_k = cw_ref.shape[0]
    x = x_ref[...]
    h = (x * lax.rsqrt(jnp.mean(x * x, axis=-1, keepdims=True) + EPS)) * g_ref[...]
    hb = h.astype(BF16)

    @pl.when(i % nj == 0)
    def _():
        xbuf[0:pad, :] = conv0_ref[...]

    @pl.when(i % nj != 0)
    def _():
        xbuf[0:pad, :] = xbuf[tm:tm + pad, :]

    (lo, hi) = segs[0]
    for c0 in range(0, hi - lo, CONV_COLS):
        cols = slice(c0, c0 + CONV_COLS)
        xbuf[pad:pad + tm, cols] = jnp.dot(hb, w_ref[:, lo + c0:lo + c0 + CONV_COLS], preferred_element_type=F32)
        acc = None
        for tap in range(conv_k):
            start = pad - (conv_k - 1) + tap
            term = cw_ref[tap:tap + 1, cols] * xbuf[start:start + tm, cols]
            acc = term if acc is None else acc + term
        out_refs[0][:, cols] = jax.nn.silu(acc)

    for (lo, hi), o_ref in zip(segs[1:], out_refs[1:-7]):
        o_ref[...] = jnp.dot(hb, w_ref[:, lo:hi], preferred_element_type=F32).astype(o_ref.dtype)
    sm_ref, smt_ref, kt_ref, vt_ref, vtb_ref, kb_ref, cst_ref = out_refs[-7:]
    sm_ref[...] = jnp.dot(hb, ws_ref[...], preferred_element_type=F32)
    smt_ref[...] = lax.dot_general(wst_ref[...], hb, _NT, preferred_element_type=F32)
    d_b = kt_ref.shape[0]
    kvt = lax.dot_general(wkvt_ref[...], hb, _NT, preferred_element_type=F32)
    kt_ref[...] = kvt[:d_b]
    vt_ref[...] = kvt[d_b:]
    vtb_ref[...] = kvt[d_b:].astype(BF16)
    kb_ref[...] = kvt[:d_b].T.astype(BF16)
    cst_ref[...] = xbuf[t_last:t_last + pad, :]


def _inproj(x2d, g, wbig, wsmall, wsmall_t, wkv_t, conv0, conv_w, widths, dtypes, tm, nb, t_valid):
    m, d = x2d.shape
    t = m // nb
    nj = t // tm
    d_b = wkv_t.shape[0] // 2
    conv_dim = conv_w.shape[1]
    t_last = (t_valid if t_valid is not None else t) - (nj - 1) * tm
    assert 0 < t_last <= tm and (t_valid is None or nj == 1)
    offs = [0]
    for w in widths:
        offs.append(offs[-1] + w)
    segs = tuple((offs[i], offs[i + 1]) for i in range(len(widths)))
    const = lambda i: (0, 0)
    seq_t = pl.BlockSpec((None, d_b, tm), lambda i: (i // nj, 0, i % nj))
    seq_c = pl.BlockSpec((None, SUBLANES, conv_dim), lambda i: (i // nj, 0, 0))
    out_shape = [jax.ShapeDtypeStruct((m, w), dt) for w, dt in zip(widths, dtypes)]
    out_shape += [jax.ShapeDtypeStruct((m, LANES), F32), jax.ShapeDtypeStruct((SMALL_COLS, m), F32),
                  jax.ShapeDtypeStruct((nb, d_b, t), F32), jax.ShapeDtypeStruct((nb, d_b, t), F32),
                  jax.ShapeDtypeStruct((nb, d_b, t), BF16), jax.ShapeDtypeStruct((m, d_b), BF16),
                  jax.ShapeDtypeStruct((nb, SUBLANES, conv_dim), F32)]
    out_specs = [pl.BlockSpec((tm, w), lambda i: (i, 0)) for w in widths]
    out_specs += [pl.BlockSpec((tm, LANES), lambda i: (i, 0)), pl.BlockSpec((SMALL_COLS, tm), lambda i: (0, i)),
                  seq_t, seq_t, seq_t, pl.BlockSpec((tm, d_b), lambda i: (i, 0)), seq_c]
    return pl.pallas_call(
        functools.partial(_inproj_body, segs=segs, nj=nj, t_last=t_last),
        grid=(m // tm,),
        in_specs=[pl.BlockSpec((tm, d), lambda i: (i, 0)),
                  pl.BlockSpec((1, d), const),
                  pl.BlockSpec(wbig.shape, const, pipeline_mode=pl.Buffered(1)),
                  pl.BlockSpec(wsmall.shape, const, pipeline_mode=pl.Buffered(1)),
                  pl.BlockSpec(wsmall_t.shape, const, pipeline_mode=pl.Buffered(1)),
                  pl.BlockSpec(wkv_t.shape, const, pipeline_mode=pl.Buffered(1)),
                  seq_c,
                  pl.BlockSpec(conv_w.shape, const)],
        out_specs=out_specs,
        out_shape=out_shape,
        scratch_shapes=[pltpu.VMEM((tm + 2 * SUBLANES, conv_dim), F32)],
        compiler_params=_params(("arbitrary",)),
        name="inproj",
    )(x2d, g.reshape(1, d), wbig, wsmall, wsmall_t, wkv_t, conv0, conv_w)


def _fgate_body(sm_ref, smt_ref, bfr_ref, bfc_ref, logf_ref, crow_ref, kaug_ref, qaug_ref,
                carry_c, carry_r, *, nh):
    j = pl.program_id(1)
    tc = sm_ref.shape[0]

    @pl.when(j == 0)
    def _():
        carry_c[...] = jnp.zeros_like(carry_c)
        carry_r[...] = jnp.zeros_like(carry_r)

    logf_c = jax.nn.log_sigmoid(sm_ref[...] + bfr_ref[...])
    tril = (_iota2((tc, tc), 0) >= _iota2((tc, tc), 1)).astype(BF16)
    ccol = _xdot(tril, logf_c) + carry_c[...]
    carry_c[...] = ccol[tc - 1:tc, :]

    pieces = _split3(ccol)
    li = _iota2((LANES, LANES), 0)
    lj = _iota2((LANES, LANES), 1)
    lane = _iota2((1, LANES), 1)
    in_heads = lane < nh * BIAS_LANES
    half = BIAS_LANES // 2
    k_aug = jnp.where(in_heads & (lane % BIAS_LANES >= half), 1.0, 0.0)
    q_aug = jnp.where(in_heads & (lane % BIAS_LANES < half), 1.0, 0.0)
    for p, piece in enumerate(pieces):
        to_k = ((lj == BIAS_LANES * li + p) & (li < nh)).astype(BF16)
        to_q = ((lj == BIAS_LANES * li + half + p) & (li < nh)).astype(BF16)
        k_aug = k_aug - jnp.dot(piece, to_k, preferred_element_type=F32)
        q_aug = q_aug + jnp.dot(piece, to_q, preferred_element_type=F32)
    kaug_ref[...] = k_aug.astype(BF16)
    qaug_ref[...] = q_aug.astype(BF16)

    logf_r = jax.nn.log_sigmoid(smt_ref[0:nh, :] + bfc_ref[...])
    logf_ref[...] = logf_r
    triu = (_iota2((tc, tc), 0) <= _iota2((tc, tc), 1)).astype(BF16)
    crow = _xdot_r(logf_r, triu) + carry_r[:, 0:1]
    crow_ref[...] = crow
    carry_r[...] = jnp.broadcast_to(crow[:, tc - 1:tc], carry_r.shape)


def _fgate(small, small_t, b_f, nb, t, tc):
    nh = b_f.shape[0]
    nj = t // tc
    bfr = jnp.zeros((1, LANES), F32).at[0, :nh].set(b_f)
    bfc = b_f.reshape(nh, 1)
    return pl.pallas_call(
        functools.partial(_fgate_body, nh=nh),
        grid=(nb, nj),
        in_specs=[pl.BlockSpec((tc, LANES), lambda b, j: (b * nj + j, 0)),
                  pl.BlockSpec((SMALL_COLS, tc), lambda b, j: (0, b * nj + j)),
                  pl.BlockSpec((1, LANES), lambda b, j: (0, 0)),
                  pl.BlockSpec((nh, 1), lambda b, j: (0, 0))],
        out_specs=[pl.BlockSpec((None, nh, tc), lambda b, j: (b, 0, j)),
                   pl.BlockSpec((None, nh, tc), lambda b, j: (b, 0, j)),
                   pl.BlockSpec((None, tc, LANES), lambda b, j: (b, j, 0)),
                   pl.BlockSpec((None, tc, LANES), lambda b, j: (b, j, 0))],
        out_shape=[jax.ShapeDtypeStruct((nb, nh, t), F32),
                   jax.ShapeDtypeStruct((nb, nh, t), F32),
                   jax.ShapeDtypeStruct((nb, t, LANES), BF16),
                   jax.ShapeDtypeStruct((nb, t, LANES), BF16)],
        scratch_shapes=[pltpu.VMEM((1, LANES), F32), pltpu.VMEM((nh, LANES), F32)],
        compiler_params=_params(("parallel", "arbitrary")),
        name="fgate",
    )(small, small_t, bfr, bfc)


def _delta_body(qkv_ref, sm_ref, za_ref, s0_ref, alog_ref, dtb_ref, og_ref,
                oa_ref, sfin_ref, s_scr, *, nh, dk, dv, t_valid, n_chunks, beta_lane):
    j = pl.program_id(1)
    lb = qkv_ref.shape[0]
    c = CHUNK

    @pl.when(j == 0)
    def _():
        s_scr[...] = s0_ref[...]

    y = qkv_ref[...]

    sm = sm_ref[...]
    beta_all = jax.nn.sigmoid(sm)
    g_all = -jnp.exp(alog_ref[...]) * jax.nn.softplus(sm + dtb_ref[...])
    if t_valid is not None:
        tok = j * lb + _iota2((lb, LANES), 0)
        valid = tok < t_valid
        beta_all = jnp.where(valid, beta_all, 0.0)
        g_all = jnp.where(valid, g_all, 0.0)

    ri = _iota2((c, c), 0)
    ci = _iota2((c, c), 1)
    tril_b = (ri >= ci).astype(BF16)
    lower = ri >= ci
    strict = ri > ci
    eye = (ri == ci).astype(F32)
    og = og_ref[...]
    qscale = dk ** -0.5
    kbase = nh * dk
    vbase = 2 * nh * dk
    a_lane = beta_lane + nh

    probs = [(ch, h) for ch in range(n_chunks) for h in range(nh)]
    strict_w = _iota2((c, LANES), 0) > _iota2((c, LANES), 1)
    qs, ks, vs, betas = {}, {}, {}, {}
    diffs, gcums = {}, {}
    for ch in range(n_chunks):
        r0 = ch * c
        cols = []
        for h in range(nh):
            q = y[r0:r0 + c, h * dk:(h + 1) * dk]
            k = y[r0:r0 + c, kbase + h * dk:kbase + (h + 1) * dk]
            qs[ch, h] = q * lax.rsqrt(jnp.sum(q * q, axis=-1, keepdims=True) + EPS) * qscale
            ks[ch, h] = k * lax.rsqrt(jnp.sum(k * k, axis=-1, keepdims=True) + EPS)
            vs[ch, h] = y[r0:r0 + c, vbase + h * dv:vbase + (h + 1) * dv]
            betas[ch, h] = beta_all[r0:r0 + c, beta_lane + h:beta_lane + h + 1]
            g = g_all[r0:r0 + c, a_lane + h:a_lane + h + 1]
            g_w = jnp.broadcast_to(g, (c, LANES))
            cols += [jnp.where(strict_w, g_w, 0.0), g_w]
        cum = _xdot(tril_b, jnp.concatenate(cols, axis=1))
        for h in range(nh):
            diffs[ch, h] = cum[:, 2 * h * LANES:2 * h * LANES + c]
            gcums[ch, h] = cum[:, (2 * h + 1) * LANES:(2 * h + 2) * LANES]

    decays, egcs, kbs, n_pows, t_invs = {}, {}, {}, {}, {}
    for p in probs:
        decays[p] = jnp.where(lower, jnp.exp(diffs[p]), 0.0)
        egcs[p] = jnp.exp(gcums[p])
        kbs[p] = ks[p] * betas[p]
    for p in probs:
        n_pows[p] = -jnp.where(strict, _bdot_nt(kbs[p], ks[p]) * decays[p], 0.0)
        t_invs[p] = eye + n_pows[p]
    span = 2
    while span < c:
        for p in probs:
            n_pows[p] = _bdot(n_pows[p], n_pows[p])
        for p in probs:
            t_invs[p] = t_invs[p] + _bdot(t_invs[p], n_pows[p])
        span *= 2
    us, ws, attns = {}, {}, {}
    for p in probs:
        uw = _bdot(t_invs[p], jnp.concatenate([vs[p] * betas[p], kbs[p] * egcs[p]], axis=1))
        us[p] = uw[:, :dv]
        ws[p] = uw[:, dv:]
    for p in probs:
        attns[p] = _bdot_nt(qs[p], ks[p]) * decays[p]

    for ch in range(n_chunks):
        r0 = ch * c
        s_olds = [s_scr[h] for h in range(nh)]
        wss = [_bdot(jnp.concatenate([ws[ch, h], qs[ch, h] * egcs[ch, h]], axis=0), s_olds[h]) for h in range(nh)]
        v_news = [us[ch, h] - wss[h][:c] for h in range(nh)]
        os_ = [wss[h][c:] + _bdot(attns[ch, h], v_news[h]) for h in range(nh)]
        for h in range(nh):
            gcum = gcums[ch, h]
            glast = gcum[c - 1:c, :]
            kd = ks[ch, h] * jnp.exp(glast - gcum)
            s_scr[h] = s_olds[h] * jnp.exp(glast[:, :dv]) + _bdot_tn(kd, v_news[h])
        for h in range(nh):
            o = os_[h]
            o = o * lax.rsqrt(jnp.mean(o * o, axis=-1, keepdims=True) + EPS) * og
            z = za_ref[r0:r0 + c, h * dv:(h + 1) * dv]
            oa_ref[r0:r0 + c, h * dv:(h + 1) * dv] = (o * jax.nn.silu(z)).astype(oa_ref.dtype)

    @pl.when(j == pl.num_programs(1) - 1)
    def _():
        sfin_ref[...] = s_scr[...]


def _delta(qkva, small, za, s0, a_log, dt_bias, onorm_g, nb, t, lb, t_valid, beta_lane):
    nh, dk, dv = s0.shape[1:]
    conv_dim = qkva.shape[1]
    nj = t // lb
    a_lane = beta_lane + nh
    alog_row = jnp.zeros((1, LANES), F32).at[0, a_lane:a_lane + nh].set(a_log)
    dtb_row = jnp.zeros((1, LANES), F32).at[0, a_lane:a_lane + nh].set(dt_bias)
    const = lambda b, j: (0, 0)
    body = functools.partial(_delta_body, nh=nh, dk=dk, dv=dv, t_valid=t_valid,
                             n_chunks=lb // CHUNK, beta_lane=beta_lane)
    return pl.pallas_call(
        body,
        grid=(nb, nj),
        in_specs=[pl.BlockSpec((lb, conv_dim), lambda b, j: (b * nj + j, 0)),
                  pl.BlockSpec((lb, LANES), lambda b, j: (b * nj + j, 0)),
                  pl.BlockSpec((lb, nh * dv), lambda b, j: (b * nj + j, 0)),
                  pl.BlockSpec((None, nh, dk, dv), lambda b, j: (b, 0, 0, 0)),
                  pl.BlockSpec((1, LANES), const),
                  pl.BlockSpec((1, LANES), const),
                  pl.BlockSpec((1, dv), const)],
        out_specs=[pl.BlockSpec((lb, nh * dv), lambda b, j: (b * nj + j, 0)),
                   pl.BlockSpec((None, nh, dk, dv), lambda b, j: (b, 0, 0, 0))],
        out_shape=[jax.ShapeDtypeStruct((nb * t, nh * dv), BF16),
                   jax.ShapeDtypeStruct((nb, nh, dk, dv), F32)],
        scratch_shapes=[pltpu.VMEM((nh, dk, dv), F32)],
        compiler_params=_params(("parallel", "arbitrary")),
        name="delta",
    )(qkva, small, za, s0, alog_row, dtb_row, onorm_g.reshape(1, dv))


def _is_pow2(x):
    return math.frexp(x)[0] == 0.5


def _fox_body(q_ref, k_ref, v_ref, kaug_ref, qaug_ref, zb_ref, ob_ref, *, nh, dh, tk, heads_per_pass):
    qi = pl.program_id(1)
    tq = q_ref.shape[0]
    scale = dh ** -0.5
    lane = _iota2((1, LANES), 1)
    frow = _iota2((LANES, 1), 0)
    per = LANES // dh
    krow = _iota2((tk, tq), 0)
    qcol = _iota2((tk, tq), 1)
    ones_rows = jnp.ones((BF16_SUBLANES, tk), BF16)
    q_aug = qaug_ref[...]

    for h0 in range(0, nh, heads_per_pass):
        heads = list(range(h0, h0 + heads_per_pass))
        gls, halves, qhs = [], [], []
        for h in heads:
            grp, hh = divmod(h, per)
            gl = slice(grp * LANES, (grp + 1) * LANES)
            in_head = (lane >= hh * dh) & (lane < (hh + 1) * dh)
            q2 = q_ref[:, gl] * jnp.asarray(scale, q_ref.dtype)
            gls.append(gl)
            halves.append(hh)
            qh = jnp.where(in_head, q2, jnp.zeros_like(q2))
            own_bias = (lane >= h * BIAS_LANES) & (lane < (h + 1) * BIAS_LANES)
            qb = jnp.where(own_bias, q_aug, jnp.zeros_like(q_aug))
            qhs.append(jnp.concatenate([qh, qb], axis=1))

        def step(jb, carry, diag=None):
            masked = diag is not None
            ks = pl.multiple_of(jb * tk, tk)
            k_aug = kaug_ref[pl.ds(ks, tk), :]
            ss = []
            for i, h in enumerate(heads):
                k2 = k_ref[pl.ds(ks, tk), gls[i]]
                ss.append(_bdot_nt(jnp.concatenate([k2, k_aug], axis=1), qhs[i]))
            ps, new = [], []
            for i, h in enumerate(heads):
                m, acc = carry[i]
                s = ss[i]
                if masked:
                    s = jnp.where(krow + diag * tk <= qcol, s, -jnp.inf)
                m_new = jnp.maximum(m, jnp.max(s, axis=0, keepdims=True))
                ps.append(jnp.exp(s - m_new).astype(BF16))
                new.append((m_new, acc * jnp.exp(m - m_new)))
            out = []
            for i, h in enumerate(heads):
                m_new, acc = new[i]
                v_ext = jnp.concatenate([v_ref[gls[i], pl.ds(ks, tk)], ones_rows], axis=0)
                out.append((m_new, acc + jnp.dot(v_ext, ps[i], preferred_element_type=F32)))
            return tuple(out)

        init = tuple((jnp.full((1, tq), -jnp.inf, F32), jnp.zeros((LANES + BF16_SUBLANES, tq), F32))
                     for _ in heads)
        per_q = tq // tk
        fin = lax.fori_loop(0, qi * per_q, step, init)
        for d in range(per_q):
            fin = step(qi * per_q + d, fin, diag=d)
        for g0 in range(0, heads_per_pass, per):
            o2 = None
            for i in range(g0, g0 + per):
                _, acc = fin[i]
                o = acc[:LANES] / acc[LANES:LANES + 1]
                keep = (frow >= halves[i] * dh) & (frow < (halves[i] + 1) * dh)
                o2 = o if o2 is None else jnp.where(keep, o, o2)
            gl = gls[g0]
            ob_ref[:, gl] = (o2.T * jax.nn.silu(zb_ref[:, gl])).astype(ob_ref.dtype)


def _fox_prompt(qb, kb, v3, k_aug, q_aug, zb, nb, t, nh, dh, tq):
    d_b = nh * dh
    nq = t // tq
    q3 = qb.reshape(nb, t, d_b)
    k3 = kb.reshape(nb, t, d_b)
    z3 = zb.reshape(nb, t, d_b)
    assert nh * BIAS_LANES <= LANES and v3.dtype == BF16 and _is_pow2(dh ** -0.5)
    out = pl.pallas_call(
        functools.partial(_fox_body, nh=nh, dh=dh, tk=min(FOX_TK, tq), heads_per_pass=FOX_HEADS_PER_PASS),
        grid=(nb, nq),
        in_specs=[pl.BlockSpec((None, tq, d_b), lambda b, i: (b, i, 0)),
                  pl.BlockSpec((None, t, d_b), lambda b, i: (b, 0, 0)),
                  pl.BlockSpec((None, d_b, t), lambda b, i: (b, 0, 0)),
                  pl.BlockSpec((None, t, LANES), lambda b, i: (b, 0, 0)),
                  pl.BlockSpec((None, tq, LANES), lambda b, i: (b, i, 0)),
                  pl.BlockSpec((None, tq, d_b), lambda b, i: (b, i, 0))],
        out_specs=pl.BlockSpec((None, tq, d_b), lambda b, i: (b, i, 0)),
        out_shape=jax.ShapeDtypeStruct((nb, t, d_b), BF16),
        compiler_params=_params(("parallel", "arbitrary")),
        name="fox_prompt",
    )(q3, k3, v3, k_aug, q_aug, z3)
    return out.reshape(nb * t, d_b)


def _fox_sample_body(pt_ref, q_ref, kn_ref, vn_ref, crow_ref, *refs, nh, dh, t_new, pages):
    del pt_ref
    k_refs = refs[0:pages]
    v_refs = refs[pages:2 * pages]
    lf_refs = refs[2 * pages:3 * pages]
    ob_ref = refs[3 * pages]
    qblk, m_scr, l_scr, acc_scr, carry_scr = refs[3 * pages + 1:]
    j = pl.program_id(1)
    d_b = nh * dh
    nrow = t_new * nh
    page = k_refs[0].shape[1]
    tp = crow_ref.shape[1]
    scale = dh ** -0.5
    head_mask = (_iota2((nh, d_b), 1) // dh) == _iota2((nh, d_b), 0)
    c_tiled = jnp.concatenate([crow_ref[...]] * t_new, axis=0)
    row_tok = _iota2((nrow, tp), 0) // nh
    lane_tok = _iota2((nrow, tp), 1)
    c_own = jnp.sum(jnp.where(lane_tok == row_tok, c_tiled, 0.0), axis=-1, keepdims=True)

    @pl.when(j == 0)
    def _():
        for t in range(t_new):
            qrow = jnp.broadcast_to(q_ref[t:t + 1, :].astype(F32), (nh, d_b))
            qblk[t * nh:(t + 1) * nh, :] = jnp.where(head_mask, qrow, 0.0)
        m_scr[...] = jnp.full(m_scr.shape, -jnp.inf, F32)
        l_scr[...] = jnp.zeros_like(l_scr)
        acc_scr[...] = jnp.zeros_like(acc_scr)
        carry_scr[...] = jnp.zeros_like(carry_scr)

    def update(s_tiles, v_blocks):
        tmax = [jnp.max(s, axis=-1, keepdims=True) for s in s_tiles]
        m_old = m_scr[...]
        m_new = functools.reduce(jnp.maximum, tmax, m_old)
        corr = jnp.exp(m_old - m_new)
        p_tiles = [jnp.exp(s - m_new) for s in s_tiles]
        psum = functools.reduce(lambda a, b: a + b, [jnp.sum(p, axis=-1, keepdims=True) for p in p_tiles])
        l_scr[...] = l_scr[...] * corr + psum
        pv = functools.reduce(lambda a, b: a + b, [_bdot_nt(p, v) for p, v in zip(p_tiles, v_blocks)])
        acc_scr[...] = acc_scr[...] * corr + pv
        m_scr[...] = m_new

    after = (_iota2((page, page), 0) > _iota2((page, page), 1)).astype(BF16)
    sums = jnp.concatenate([after, jnp.ones((page, page), BF16)], axis=1)
    lf = jnp.concatenate([lf_refs[g][...] for g in range(pages)], axis=0)
    cum = _xdot_r(lf, sums)
    run = carry_scr[...]
    biases = []
    for g in range(pages):
        biases.append(cum[g * nh:(g + 1) * nh, :page] + run)
        run = run + cum[g * nh:(g + 1) * nh, page:]
    carry_scr[...] = run

    qb = qblk[...].astype(BF16)
    raw = [jnp.dot(qb, k_refs[g][...].astype(BF16), preferred_element_type=F32) for g in range(pages)]
    s_tiles = [raw[g] * scale + jnp.concatenate([biases[g]] * t_new, axis=0) + c_own for g in range(pages)]
    update(s_tiles, [v_refs[g][...] for g in range(pages)])

    @pl.when(j == pl.num_programs(1) - 1)
    def _():
        s = jnp.dot(qb, kn_ref[...].astype(BF16), preferred_element_type=F32) * scale + c_own - c_tiled
        s = jnp.where(lane_tok <= row_tok, s, -jnp.inf)
        update([s], [vn_ref[...]])
        o = acc_scr[...] / l_scr[...]
        ob_ref[...] = jnp.zeros_like(ob_ref)
        for t in range(t_new):
            blk = jnp.where(head_mask, o[t * nh:(t + 1) * nh, :], 0.0)
            ob_ref[t:t + 1, :] = jnp.sum(blk, axis=0, keepdims=True)


def _fox_sample(qb, k3, v3, crow, cache_kt, cache_vt, cache_lft, page_table, layer, nb, tp, t_new, pages):
    d_b, page = cache_kt.shape[2:]
    nh = cache_lft.shape[2]
    dh = d_b // nh
    n_pages = page_table.shape[1]
    nj = n_pages // pages
    q3 = qb.reshape(nb, tp, d_b)

    def page_map(g):
        return lambda b, j, pt: (layer, pt[b, n_pages - 1 - (j * pages + g)], 0, 0)

    seq = lambda b, j, pt: (b, 0, 0)
    in_specs = [pl.BlockSpec((None, tp, d_b), seq),
                pl.BlockSpec((None, d_b, tp), seq),
                pl.BlockSpec((None, d_b, tp), seq),
                pl.BlockSpec((None, nh, tp), seq)]
    in_specs += [pl.BlockSpec((None, None, d_b, page), page_map(g)) for g in range(pages)]
    in_specs += [pl.BlockSpec((None, None, d_b, page), page_map(g)) for g in range(pages)]
    in_specs += [pl.BlockSpec((None, None, nh, page), page_map(g)) for g in range(pages)]
    nrow = t_new * nh
    grid_spec = pltpu.PrefetchScalarGridSpec(
        num_scalar_prefetch=1,
        grid=(nb, nj),
        in_specs=in_specs,
        out_specs=pl.BlockSpec((None, tp, d_b), seq),
        scratch_shapes=[pltpu.VMEM((nrow, d_b), F32),
                        pltpu.VMEM((nrow, 1), F32),
                        pltpu.VMEM((nrow, 1), F32),
                        pltpu.VMEM((nrow, d_b), F32),
                        pltpu.VMEM((nh, page), F32)],
    )
    out = pl.pallas_call(
        functools.partial(_fox_sample_body, nh=nh, dh=dh, t_new=t_new, pages=pages),
        grid_spec=grid_spec,
        out_shape=jax.ShapeDtypeStruct((nb, tp, d_b), F32),
        compiler_params=_params(("parallel", "arbitrary")),
        name="fox_sample",
    )(page_table, q3, k3, v3, crow, *([cache_kt] * pages), *([cache_vt] * pages), *([cache_lft] * pages))
    return out.reshape(nb * tp, d_b)


def _merge_body(x_ref, oa_ref, ob_ref, ga_ref, gb_ref, wpa_ref, wpb_ref, wo_ref, fg_ref, *rest, final_norm, gate_b):
    ob = ob_ref[...]
    if gate_b:
        zb_ref, y_ref = rest
        ob = ob * jax.nn.silu(zb_ref[...])
    else:
        (y_ref,) = rest
    ya = jnp.dot(oa_ref[...].astype(BF16), wpa_ref[...], preferred_element_type=F32)
    yb = jnp.dot(ob.astype(BF16), wpb_ref[...], preferred_element_type=F32)
    merged = jax.nn.sigmoid(ga_ref[...]) * ya + jax.nn.sigmoid(gb_ref[...]) * yb
    y = x_ref[...] + jnp.dot(merged.astype(BF16), wo_ref[...], preferred_element_type=F32)
    if final_norm:
        y = (y * lax.rsqrt(jnp.mean(y * y, axis=-1, keepdims=True) + EPS)) * fg_ref[...]
    y_ref[...] = y


def _merge(x2d, oa, ob, zb, ga, gb, wpa, wpb, wo, final_g, final_norm, tm):
    m, d = x2d.shape
    row = lambda w: pl.BlockSpec((tm, w), lambda i: (i, 0))
    const = lambda i: (0, 0)
    gate_b = zb is not None
    in_specs = [row(d), row(oa.shape[1]), row(ob.shape[1]), row(d), row(d),
                pl.BlockSpec(wpa.shape, const), pl.BlockSpec(wpb.shape, const), pl.BlockSpec(wo.shape, const),
                pl.BlockSpec((1, d), const)]
    args = [x2d, oa, ob, ga, gb, wpa, wpb, wo, final_g.reshape(1, d)]
    if gate_b:
        in_specs.append(row(zb.shape[1]))
        args.append(zb)
    return pl.pallas_call(
        functools.partial(_merge_body, final_norm=final_norm, gate_b=gate_b),
        grid=(m // tm,),
        in_specs=in_specs,
        out_specs=row(d),
        out_shape=jax.ShapeDtypeStruct((m, d), F32),
        compiler_params=_params(("parallel",)),
        name="merge",
    )(*args)


def _pack_in_weights(w_in_l, conv_dim, d_a, nh_a, d_b, nh_b, d_model):
    sizes = (conv_dim, d_a, nh_a, nh_a, 3 * d_b, nh_b, d_b, d_model, d_model)
    offs = [0]
    for s in sizes:
        offs.append(offs[-1] + s)
    col = lambda i: w_in_l[:, offs[i]:offs[i + 1]]
    qkv_b = col(4)
    wbig = jnp.concatenate([col(0), col(1), qkv_b[:, :d_b], col(6), col(7), col(8)], axis=1).astype(BF16)
    small = jnp.concatenate([col(5), col(2), col(3)], axis=1)
    wsmall = jnp.pad(small, ((0, 0), (0, LANES - small.shape[1]))).astype(BF16)
    wsmall_t = small.T.astype(BF16)
    wkv_t = qkv_b[:, d_b:].T.astype(BF16)
    return wbig, wsmall, wsmall_t, wkv_t


def _layer(x2d, nb, t, t_valid, conv0, s0, attend, lw, final_g, final_norm, tm, lb, tc):
    (norm_g, w_in, conv_w, a_log, dt_bias, onorm_g, b_f, w_pa, w_pb, w_o) = lw
    d_model = x2d.shape[1]
    nh_a, dk, dv = s0.shape[1:]
    conv_dim = conv_w.shape[1]
    d_a = nh_a * dv
    nh_b = b_f.shape[0]
    d_b = w_pb.shape[0]
    wbig, wsmall, wsmall_t, wkv_t = _pack_in_weights(w_in, conv_dim, d_a, nh_a, d_b, nh_b, d_model)
    widths = (conv_dim, d_a, d_b, d_b, d_model, d_model)
    dtypes = (F32, F32, BF16, F32, F32, F32)
    qkva, za, qb, zb, ga, gb, small, small_t, kt, vt, vtb, kb, conv_tail = _inproj(
        x2d, norm_g, wbig, wsmall, wsmall_t, wkv_t, conv0, conv_w, widths, dtypes, tm, nb, t_valid)
    logf_t, crow, k_aug, q_aug = _fgate(small, small_t, b_f, nb, t, tc)
    oa, s_fin = _delta(qkva, small, za, s0, a_log, dt_bias, onorm_g, nb, t, lb, t_valid, nh_b)
    ob, ob_gated = attend(qb, kb, kt, vt, vtb, crow, k_aug, q_aug, zb)
    y = _merge(x2d, oa, ob, None if ob_gated else zb, ga, gb, w_pa.astype(BF16), w_pb.astype(BF16),
               w_o.astype(BF16), final_g, final_norm, tm)
    return y, (kt, vt, logf_t, s_fin, conv_tail)


def kernel(x_prompt, x_sample, cache_k, cache_v, cache_logf, state_delta, state_conv, page_table, norm_g, w_in,
           conv_w, A_log, dt_bias, onorm_g, b_f, w_pa, w_pb, w_o, final_g):
    depth = norm_g.shape[0]
    nbp, tp_len, d_model = x_prompt.shape
    nbs, ts_len, _ = x_sample.shape
    nh_a, dk, dv = state_delta.shape[2:]
    conv_k, conv_dim = conv_w.shape[1:]
    nh_b, dh_b = cache_k.shape[3:]
    d_b = nh_b * dh_b
    ts_pad = LANES
    assert ts_len <= CHUNK and tp_len % LANES == 0 and conv_k - 1 <= SUBLANES and ts_len >= conv_k - 1

    hp = x_prompt.reshape(nbp * tp_len, d_model)
    hs = jnp.pad(x_sample, ((0, 0), (0, ts_pad - ts_len), (0, 0))).reshape(nbs * ts_pad, d_model)
    conv0_p = jnp.zeros((nbp, SUBLANES, conv_dim), F32)
    s0_p = jnp.zeros((nbp, nh_a, dk, dv), F32)
    n_pages = page_table.shape[1]
    page = cache_k.shape[2]
    assert dk == LANES and dv == LANES
    cache_kt = jnp.transpose(cache_k, (0, 1, 3, 4, 2)).reshape(depth, cache_k.shape[1], d_b, page)
    cache_vt = jnp.transpose(cache_v, (0, 1, 3, 4, 2)).reshape(depth, cache_v.shape[1], d_b, page)
    cache_lft = jnp.transpose(cache_logf, (0, 1, 3, 2))
    pages = next(g for g in (SAMPLE_PAGES_PER_STEP, 8, 4, 2, 1) if n_pages % g == 0)
    tq = 256 if tp_len % 256 == 0 else LANES
    lb_p = 256 if tp_len % 256 == 0 else LANES
    tm_p = next(r for r in (512, 256, LANES) if tp_len % r == 0)
    tm_s = ts_pad
    tc_p = 512 if tp_len % 512 == 0 else LANES

    st_p, st_s = [], []
    for l in range(depth):
        lw = (norm_g[l], w_in[l], conv_w[l], A_log[l], dt_bias[l], onorm_g[l], b_f[l], w_pa[l], w_pb[l], w_o[l])
        last = l == depth - 1

        def attend_p(qb, kb, kt, vt, vtb, crow, k_aug, q_aug, zb):
            del kt, vt, crow
            return _fox_prompt(qb, kb, vtb, k_aug, q_aug, zb, nb=nbp, t=tp_len, nh=nh_b, dh=dh_b, tq=tq), True

        hp, sp = _layer(hp, nbp, tp_len, None, conv0_p, s0_p, attend_p, lw, final_g, last, tm_p, lb_p, tc_p)

        conv0_s = jnp.pad(state_conv[l], ((0, 0), (SUBLANES - (conv_k - 1), 0), (0, 0)))

        def attend_s(qb, kb, kt, vt, vtb, crow, k_aug, q_aug, zb, l=l):
            del kb, vtb, k_aug, q_aug, zb
            return _fox_sample(qb, kt, vt, crow, cache_kt, cache_vt, cache_lft, page_table, l, nbs, ts_pad,
                               ts_len, pages), False

        hs, ss = _layer(hs, nbs, ts_pad, ts_len, conv0_s, state_delta[l], attend_s, lw, final_g, last,
                        tm_s, ts_pad, ts_pad)
        st_p.append(sp)
        st_s.append(ss)

    def seq_view(a, nb, t, keep):
        return a.reshape(nb, t, *a.shape[1:])[:, :keep]

    def token_major(feat_major, nb, t, keep):
        a = feat_major.reshape(depth, nb, nh_b, dh_b, t)[..., :keep]
        return jnp.transpose(a, (0, 1, 4, 2, 3))

    y_prompt = hp.reshape(nbp, tp_len, d_model)
    y_sample = seq_view(hs, nbs, ts_pad, ts_len)
    outs_p, outs_s = [], []
    for (st, nb, t, keep, outs) in ((st_p, nbp, tp_len, tp_len, outs_p), (st_s, nbs, ts_pad, ts_len, outs_s)):
        k_all = token_major(jnp.stack([s[0] for s in st]), nb, t, keep)
        v_all = token_major(jnp.stack([s[1] for s in st]), nb, t, keep)
        logf_all = jnp.transpose(jnp.stack([s[2] for s in st])[..., :keep], (0, 1, 3, 2))
        delta_all = jnp.stack([s[3] for s in st])
        conv_all = jnp.stack([s[4][:, SUBLANES - (conv_k - 1):] for s in st])
        outs.extend([k_all, v_all, logf_all, delta_all, conv_all])
    return (y_prompt, y_sample, *outs_p, *outs_s)
```

```python
import functools
import math

import jax
import jax.numpy as jnp
from jax import lax
from jax.experimental import pallas as pl
from jax.experimental.pallas import tpu as pltpu

F32 = jnp.float32
BF16 = jnp.bfloat16
EPS = 1e-6
CHUNK = 64
LANES = 128
SUBLANES = 8
BF16_SUBLANES = 16
SMALL_COLS = 16
VMEM_LIMIT = 56 * 1024 * 1024
FOX_HEADS_PER_PASS = 8
FOX_TK = 256
SAMPLE_PAGES_PER_STEP = 32
CONV_COLS = 256
BIAS_LANES = 6

_NT = (((1,), (1,)), ((), ()))
_TN = (((0,), (0,)), ((), ()))


def _bdot(a, b):
    return jnp.dot(a.astype(BF16), b.astype(BF16), preferred_element_type=F32)


def _bdot_nt(a, b):
    return lax.dot_general(a.astype(BF16), b.astype(BF16), _NT, preferred_element_type=F32)


def _bdot_tn(a, b):
    return lax.dot_general(a.astype(BF16), b.astype(BF16), _TN, preferred_element_type=F32)


def _split3(x):
    x1 = x.astype(BF16)
    r1 = x - x1.astype(F32)
    x2 = r1.astype(BF16)
    x3 = (r1 - x2.astype(F32)).astype(BF16)
    return x1, x2, x3


def _xdot(mask_bf16, x, dims=None):
    parts = _split3(x)
    if dims is None:
        return sum(jnp.dot(mask_bf16, p, preferred_element_type=F32) for p in parts)
    return sum(lax.dot_general(mask_bf16, p, dims, preferred_element_type=F32) for p in parts)


def _xdot_r(x, mask_bf16, dims=None):
    parts = _split3(x)
    if dims is None:
        return sum(jnp.dot(p, mask_bf16, preferred_element_type=F32) for p in parts)
    return sum(lax.dot_general(p, mask_bf16, dims, preferred_element_type=F32) for p in parts)


def _iota2(shape, dim):
    return lax.broadcasted_iota(jnp.int32, shape, dim)


def _params(sem):
    return pltpu.CompilerParams(dimension_semantics=sem, vmem_limit_bytes=VMEM_LIMIT)


def _inproj_body(x_ref, g_ref, w_ref, ws_ref, wst_ref, wkvt_ref, conv0_ref, cw_ref, bfr_ref, bfc_ref, *refs,
                 segs, nj, t_last, nh, n_prev):
    prev_refs, refs = (refs[:2], refs[2:]) if n_prev else ((), refs)
    out_refs, (xbuf, carry_c, carry_r) = refs[:-3], refs[-3:]
    i = pl.program_id(0)
    tm = x_ref.shape[0]
    pad = SUBLANES
    conv_k = cw_ref.shape[0]
    x = x_ref[...]
    h = (x * lax.rsqrt(jnp.mean(x * x, axis=-1, keepdims=True) + EPS)) * g_ref[...]
    hb = h.astype(BF16)

    @pl.when(i % nj == 0)
    def _():
        xbuf[0:pad, :] = conv0_ref[...]

    @pl.when(i % nj != 0)
    def _():
        xbuf[0:pad, :] = xbuf[tm:tm + pad, :]

    n_tail = 10
    plain = [(o_ref, lo, c0, min(CONV_COLS, hi - lo - c0))
             for (lo, hi), o_ref in zip(segs[1:], out_refs[1:-n_tail]) for c0 in range(0, hi - lo, CONV_COLS)]

    def project(o_ref, lo, c0, width):
        o_ref[:, c0:c0 + width] = jnp.dot(hb, w_ref[:, lo + c0:lo + c0 + width],
                                          preferred_element_type=F32).astype(o_ref.dtype)

    (lo, hi) = segs[0]
    n_conv = (hi - lo) // CONV_COLS
    per_conv = -(-len(plain) // n_conv)
    for blk in range(n_conv):
        c0 = blk * CONV_COLS
        cols = slice(c0, c0 + CONV_COLS)
        xbuf[pad:pad + tm, cols] = jnp.dot(hb, w_ref[:, lo + c0:lo + c0 + CONV_COLS], preferred_element_type=F32)
        for task in plain[blk * per_conv:(blk + 1) * per_conv]:
            project(*task)
        acc = None
        for tap in range(conv_k):
            start = pad - (conv_k - 1) + tap
            term = cw_ref[tap:tap + 1, cols] * xbuf[start:start + tm, cols]
            acc = term if acc is None else acc + term
        out_refs[0][:, cols] = jax.nn.silu(acc)
    sm_ref, logf_ref, crow_ref, kaug_ref, qaug_ref, kt_ref, vt_ref, vtb_ref, kb_ref, cst_ref = out_refs[-n_tail:]
    small = jnp.dot(hb, ws_ref[...], preferred_element_type=F32)
    small_t = lax.dot_general(wst_ref[...], hb, _NT, preferred_element_type=F32)
    sm_ref[...] = small

    @pl.when(i % nj == 0)
    def _():
        carry_c[...] = jnp.zeros_like(carry_c)
        carry_r[...] = jnp.zeros_like(carry_r)

    logf_c = jax.nn.log_sigmoid(small + bfr_ref[...])
    tril = (_iota2((tm, tm), 0) >= _iota2((tm, tm), 1)).astype(BF16)
    ccol = _xdot(tril, logf_c) + carry_c[...]
    carry_c[...] = ccol[tm - 1:tm, :]
    li = _iota2((LANES, LANES), 0)
    lj = _iota2((LANES, LANES), 1)
    lane = _iota2((1, LANES), 1)
    in_heads = lane < nh * BIAS_LANES
    half = BIAS_LANES // 2
    k_aug = jnp.where(in_heads & (lane % BIAS_LANES >= half), 1.0, 0.0)
    q_aug = jnp.where(in_heads & (lane % BIAS_LANES < half), 1.0, 0.0)
    for p, piece in enumerate(_split3(ccol)):
        to_k = ((lj == BIAS_LANES * li + p) & (li < nh)).astype(BF16)
        to_q = ((lj == BIAS_LANES * li + half + p) & (li < nh)).astype(BF16)
        k_aug = k_aug - jnp.dot(piece, to_k, preferred_element_type=F32)
        q_aug = q_aug + jnp.dot(piece, to_q, preferred_element_type=F32)
    kaug_ref[...] = k_aug.astype(BF16)
    qaug_ref[...] = q_aug.astype(BF16)
    logf_r = jax.nn.log_sigmoid(small_t[0:nh, :] + bfc_ref[...])
    logf_ref[...] = logf_r
    triu = (_iota2((tm, tm), 0) <= _iota2((tm, tm), 1)).astype(BF16)
    crow = _xdot_r(logf_r, triu) + carry_r[:, 0:1]
    crow_ref[...] = crow
    carry_r[...] = jnp.broadcast_to(crow[:, tm - 1:tm], carry_r.shape)

    d_b = vtb_ref.shape[0]
    kvt = lax.dot_general(wkvt_ref[...], hb, _NT, preferred_element_type=F32)
    if n_prev:
        kt_ref[0:n_prev] = prev_refs[0][...]
        vt_ref[0:n_prev] = prev_refs[1][...]
        kt_ref[n_prev] = kvt[:d_b]
        vt_ref[n_prev] = kvt[d_b:]
    else:
        kt_ref[...] = kvt[:d_b]
        vt_ref[...] = kvt[d_b:]
    vtb_ref[...] = kvt[d_b:].astype(BF16)
    kb_ref[...] = kvt[:d_b].T.astype(BF16)
    cst_ref[...] = xbuf[t_last:t_last + pad, :]


def _inproj(x2d, g, wbig, wsmall, wsmall_t, wkv_t, conv0, conv_w, b_f, widths, dtypes, tm, nb, t_valid, prev_kv):
    m, d = x2d.shape
    t = m // nb
    nj = t // tm
    d_b = wkv_t.shape[0] // 2
    conv_dim = conv_w.shape[1]
    nh = b_f.shape[0]
    bfr = jnp.zeros((1, LANES), F32).at[0, :nh].set(b_f)
    bfc = b_f.reshape(nh, 1)
    t_last = (t_valid if t_valid is not None else t) - (nj - 1) * tm
    assert 0 < t_last <= tm and (t_valid is None or nj == 1)
    offs = [0]
    for w in widths:
        offs.append(offs[-1] + w)
    segs = tuple((offs[i], offs[i + 1]) for i in range(len(widths)))
    const = lambda i: (0, 0)
    seq_t = pl.BlockSpec((None, d_b, tm), lambda i: (i // nj, 0, i % nj))
    seq_h = pl.BlockSpec((None, nh, tm), lambda i: (i // nj, 0, i % nj))
    seq_l = pl.BlockSpec((None, tm, LANES), lambda i: (i // nj, i % nj, 0))
    seq_c = pl.BlockSpec((None, SUBLANES, conv_dim), lambda i: (i // nj, 0, 0))
    n_prev = 0 if prev_kv is None else prev_kv[0].shape[0]
    if n_prev:
        kv_shape = jax.ShapeDtypeStruct((n_prev + 1, nb, d_b, t), F32)
        kv_spec = pl.BlockSpec((n_prev + 1, None, d_b, tm), lambda i: (0, i // nj, 0, i % nj))
        prev_spec = pl.BlockSpec((n_prev, None, d_b, tm), lambda i: (0, i // nj, 0, i % nj))
    else:
        kv_shape, kv_spec = jax.ShapeDtypeStruct((nb, d_b, t), F32), seq_t
    out_shape = [jax.ShapeDtypeStruct((m, w), dt) for w, dt in zip(widths, dtypes)]
    out_shape += [jax.ShapeDtypeStruct((m, LANES), F32),
                  jax.ShapeDtypeStruct((nb, nh, t), F32), jax.ShapeDtypeStruct((nb, nh, t), F32),
                  jax.ShapeDtypeStruct((nb, t, LANES), BF16), jax.ShapeDtypeStruct((nb, t, LANES), BF16),
                  kv_shape, kv_shape,
                  jax.ShapeDtypeStruct((nb, d_b, t), BF16), jax.ShapeDtypeStruct((m, d_b), BF16),
                  jax.ShapeDtypeStruct((nb, SUBLANES, conv_dim), F32)]
    out_specs = [pl.BlockSpec((tm, w), lambda i: (i, 0)) for w in widths]
    out_specs += [pl.BlockSpec((tm, LANES), lambda i: (i, 0)), seq_h, seq_h, seq_l, seq_l,
                  kv_spec, kv_spec, seq_t, pl.BlockSpec((tm, d_b), lambda i: (i, 0)), seq_c]
    in_specs = [pl.BlockSpec((tm, d), lambda i: (i, 0)),
                pl.BlockSpec((1, d), const),
                pl.BlockSpec(wbig.shape, const, pipeline_mode=pl.Buffered(1)),
                pl.BlockSpec(wsmall.shape, const, pipeline_mode=pl.Buffered(1)),
                pl.BlockSpec(wsmall_t.shape, const, pipeline_mode=pl.Buffered(1)),
                pl.BlockSpec(wkv_t.shape, const, pipeline_mode=pl.Buffered(1)),
                seq_c,
                pl.BlockSpec(conv_w.shape, const),
                pl.BlockSpec((1, LANES), const),
                pl.BlockSpec((nh, 1), const)]
    args = [x2d, g.reshape(1, d), wbig, wsmall, wsmall_t, wkv_t, conv0, conv_w, bfr, bfc]
    if n_prev:
        in_specs += [prev_spec, prev_spec]
        args += list(prev_kv)
    return pl.pallas_call(
        functools.partial(_inproj_body, segs=segs, nj=nj, t_last=t_last, nh=nh, n_prev=n_prev),
        grid=(m // tm,),
        in_specs=in_specs,
        out_specs=out_specs,
        out_shape=out_shape,
        scratch_shapes=[pltpu.VMEM((tm + 2 * SUBLANES, conv_dim), F32),
                        pltpu.VMEM((1, LANES), F32),
                        pltpu.VMEM((nh, LANES), F32)],
        compiler_params=_params(("arbitrary",)),
        name="inproj",
    )(*args)


def _delta_body(qkv_ref, sm_ref, za_ref, s0_ref, alog_ref, dtb_ref, og_ref,
                oa_ref, sfin_ref, s_scr, *, nh, dk, dv, t_valid, n_chunks, beta_lane):
    j = pl.program_id(1)
    lb = qkv_ref.shape[0]
    c = CHUNK

    @pl.when(j == 0)
    def _():
        s_scr[...] = s0_ref[...]

    y = qkv_ref[...]

    sm = sm_ref[...]
    beta_all = jax.nn.sigmoid(sm)
    g_all = -jnp.exp(alog_ref[...]) * jax.nn.softplus(sm + dtb_ref[...])
    if t_valid is not None:
        tok = j * lb + _iota2((lb, LANES), 0)
        valid = tok < t_valid
        beta_all = jnp.where(valid, beta_all, 0.0)
        g_all = jnp.where(valid, g_all, 0.0)

    ri = _iota2((c, c), 0)
    ci = _iota2((c, c), 1)
    tril_b = (ri >= ci).astype(BF16)
    lower = ri >= ci
    strict = ri > ci
    eye = (ri == ci).astype(F32)
    og = og_ref[...]
    qscale = dk ** -0.5
    kbase = nh * dk
    vbase = 2 * nh * dk
    a_lane = beta_lane + nh

    probs = [(ch, h) for ch in range(n_chunks) for h in range(nh)]
    strict_w = _iota2((c, LANES), 0) > _iota2((c, LANES), 1)
    qs, ks, vs, betas = {}, {}, {}, {}
    diffs, gcums = {}, {}
    for ch in range(n_chunks):
        r0 = ch * c
        cols = []
        for h in range(nh):
            q = y[r0:r0 + c, h * dk:(h + 1) * dk]
            k = y[r0:r0 + c, kbase + h * dk:kbase + (h + 1) * dk]
            qs[ch, h] = q * lax.rsqrt(jnp.sum(q * q, axis=-1, keepdims=True) + EPS) * qscale
            ks[ch, h] = k * lax.rsqrt(jnp.sum(k * k, axis=-1, keepdims=True) + EPS)
            vs[ch, h] = y[r0:r0 + c, vbase + h * dv:vbase + (h + 1) * dv]
            betas[ch, h] = beta_all[r0:r0 + c, beta_lane + h:beta_lane + h + 1]
            g = g_all[r0:r0 + c, a_lane + h:a_lane + h + 1]
            g_w = jnp.broadcast_to(g, (c, LANES))
            cols += [jnp.where(strict_w, g_w, 0.0), g_w]
        cum = _xdot(tril_b, jnp.concatenate(cols, axis=1))
        for h in range(nh):
            diffs[ch, h] = cum[:, 2 * h * LANES:2 * h * LANES + c]
            gcums[ch, h] = cum[:, (2 * h + 1) * LANES:(2 * h + 2) * LANES]

    decays, egcs, kbs, n_pows, t_invs = {}, {}, {}, {}, {}
    for p in probs:
        decays[p] = jnp.where(lower, jnp.exp(diffs[p]), 0.0)
        egcs[p] = jnp.exp(gcums[p])
        kbs[p] = ks[p] * betas[p]
    for p in probs:
        n_pows[p] = -jnp.where(strict, _bdot_nt(kbs[p], ks[p]) * decays[p], 0.0)
        t_invs[p] = eye + n_pows[p]
    span = 2
    while span < c:
        for p in probs:
            n_pows[p] = _bdot(n_pows[p], n_pows[p])
        for p in probs:
            t_invs[p] = t_invs[p] + _bdot(t_invs[p], n_pows[p])
        span *= 2
    us, ws, attns = {}, {}, {}
    for p in probs:
        uw = _bdot(t_invs[p], jnp.concatenate([vs[p] * betas[p], kbs[p] * egcs[p]], axis=1))
        us[p] = uw[:, :dv]
        ws[p] = uw[:, dv:]
    for p in probs:
        attns[p] = _bdot_nt(qs[p], ks[p]) * decays[p]

    for ch in range(n_chunks):
        r0 = ch * c
        s_olds = [s_scr[h] for h in range(nh)]
        wss = [_bdot(jnp.concatenate([ws[ch, h], qs[ch, h] * egcs[ch, h]], axis=0), s_olds[h]) for h in range(nh)]
        v_news = [us[ch, h] - wss[h][:c] for h in range(nh)]
        os_ = [wss[h][c:] + _bdot(attns[ch, h], v_news[h]) for h in range(nh)]
        for h in range(nh):
            gcum = gcums[ch, h]
            glast = gcum[c - 1:c, :]
            kd = ks[ch, h] * jnp.exp(glast - gcum)
            s_scr[h] = s_olds[h] * jnp.exp(glast[:, :dv]) + _bdot_tn(kd, v_news[h])
        for h in range(nh):
            o = os_[h]
            o = o * lax.rsqrt(jnp.mean(o * o, axis=-1, keepdims=True) + EPS) * og
            z = za_ref[r0:r0 + c, h * dv:(h + 1) * dv]
            oa_ref[r0:r0 + c, h * dv:(h + 1) * dv] = (o * jax.nn.silu(z)).astype(oa_ref.dtype)

    @pl.when(j == pl.num_programs(1) - 1)
    def _():
        sfin_ref[...] = s_scr[...]


def _delta(qkva, small, za, s0, a_log, dt_bias, onorm_g, nb, t, lb, t_valid, beta_lane):
    nh, dk, dv = s0.shape[1:]
    conv_dim = qkva.shape[1]
    nj = t // lb
    a_lane = beta_lane + nh
    alog_row = jnp.zeros((1, LANES), F32).at[0, a_lane:a_lane + nh].set(a_log)
    dtb_row = jnp.zeros((1, LANES), F32).at[0, a_lane:a_lane + nh].set(dt_bias)
    const = lambda b, j: (0, 0)
    body = functools.partial(_delta_body, nh=nh, dk=dk, dv=dv, t_valid=t_valid,
                             n_chunks=lb // CHUNK, beta_lane=beta_lane)
    return pl.pallas_call(
        body,
        grid=(nb, nj),
        in_specs=[pl.BlockSpec((lb, conv_dim), lambda b, j: (b * nj + j, 0)),
                  pl.BlockSpec((lb, LANES), lambda b, j: (b * nj + j, 0)),
                  pl.BlockSpec((lb, nh * dv), lambda b, j: (b * nj + j, 0)),
                  pl.BlockSpec((None, nh, dk, dv), lambda b, j: (b, 0, 0, 0)),
                  pl.BlockSpec((1, LANES), const),
                  pl.BlockSpec((1, LANES), const),
                  pl.BlockSpec((1, dv), const)],
        out_specs=[pl.BlockSpec((lb, nh * dv), lambda b, j: (b * nj + j, 0)),
                   pl.BlockSpec((None, nh, dk, dv), lambda b, j: (b, 0, 0, 0))],
        out_shape=[jax.ShapeDtypeStruct((nb * t, nh * dv), BF16),
                   jax.ShapeDtypeStruct((nb, nh, dk, dv), F32)],
        scratch_shapes=[pltpu.VMEM((nh, dk, dv), F32)],
        compiler_params=_params(("parallel", "arbitrary")),
        name="delta",
    )(qkva, small, za, s0, alog_row, dtb_row, onorm_g.reshape(1, dv))


def _is_pow2(x):
    return math.frexp(x)[0] == 0.5


def _fox_body(q_ref, k_ref, v_ref, kaug_ref, qaug_ref, zb_ref, ob_ref, *, nh, dh, tk, heads_per_pass):
    qi = pl.program_id(1)
    tq = q_ref.shape[0]
    scale = dh ** -0.5
    lane = _iota2((1, LANES), 1)
    frow = _iota2((LANES, 1), 0)
    per = LANES // dh
    krow = _iota2((tk, tq), 0)
    qcol = _iota2((tk, tq), 1)
    ones_rows = jnp.ones((BF16_SUBLANES, tk), BF16)
    q_aug = qaug_ref[...]

    for h0 in range(0, nh, heads_per_pass):
        heads = list(range(h0, h0 + heads_per_pass))
        gls, halves, qhs = [], [], []
        for h in heads:
            grp, hh = divmod(h, per)
            gl = slice(grp * LANES, (grp + 1) * LANES)
            in_head = (lane >= hh * dh) & (lane < (hh + 1) * dh)
            q2 = q_ref[:, gl] * jnp.asarray(scale, q_ref.dtype)
            gls.append(gl)
            halves.append(hh)
            qh = jnp.where(in_head, q2, jnp.zeros_like(q2))
            own_bias = (lane >= h * BIAS_LANES) & (lane < (h + 1) * BIAS_LANES)
            qb = jnp.where(own_bias, q_aug, jnp.zeros_like(q_aug))
            qhs.append(jnp.concatenate([qh, qb], axis=1))

        def step(jb, carry, diag=None):
            masked = diag is not None
            ks = pl.multiple_of(jb * tk, tk)
            k_aug = kaug_ref[pl.ds(ks, tk), :]
            ss = []
            for i, h in enumerate(heads):
                k2 = k_ref[pl.ds(ks, tk), gls[i]]
                ss.append(_bdot_nt(jnp.concatenate([k2, k_aug], axis=1), qhs[i]))
            ps, new = [], []
            for i, h in enumerate(heads):
                m, acc = carry[i]
                s = ss[i]
                if masked:
                    s = jnp.where(krow + diag * tk <= qcol, s, -jnp.inf)
                m_new = jnp.maximum(m, jnp.max(s, axis=0, keepdims=True))
                ps.append(jnp.exp(s - m_new).astype(BF16))
                new.append((m_new, acc * jnp.exp(m - m_new)))
            out = []
            for i, h in enumerate(heads):
                m_new, acc = new[i]
                v_ext = jnp.concatenate([v_ref[gls[i], pl.ds(ks, tk)], ones_rows], axis=0)
                out.append((m_new, acc + jnp.dot(v_ext, ps[i], preferred_element_type=F32)))
            return tuple(out)

        init = tuple((jnp.full((1, tq), -jnp.inf, F32), jnp.zeros((LANES + BF16_SUBLANES, tq), F32))
                     for _ in heads)
        per_q = tq // tk
        fin = lax.fori_loop(0, qi * per_q, step, init)
        for d in range(per_q):
            fin = step(qi * per_q + d, fin, diag=d)
        for g0 in range(0, heads_per_pass, per):
            o2 = None
            for i in range(g0, g0 + per):
                _, acc = fin[i]
                o = acc[:LANES] / acc[LANES:LANES + 1]
                keep = (frow >= halves[i] * dh) & (frow < (halves[i] + 1) * dh)
                o2 = o if o2 is None else jnp.where(keep, o, o2)
            gl = gls[g0]
            ob_ref[:, gl] = (o2.T * jax.nn.silu(zb_ref[:, gl])).astype(ob_ref.dtype)


def _fox_prompt(qb, kb, v3, k_aug, q_aug, zb, nb, t, nh, dh, tq):
    d_b = nh * dh
    nq = t // tq
    q3 = qb.reshape(nb, t, d_b)
    k3 = kb.reshape(nb, t, d_b)
    z3 = zb.reshape(nb, t, d_b)
    assert nh * BIAS_LANES <= LANES and v3.dtype == BF16 and _is_pow2(dh ** -0.5)
    out = pl.pallas_call(
        functools.partial(_fox_body, nh=nh, dh=dh, tk=min(FOX_TK, tq), heads_per_pass=FOX_HEADS_PER_PASS),
        grid=(nb, nq),
        in_specs=[pl.BlockSpec((None, tq, d_b), lambda b, i: (b, i, 0)),
                  pl.BlockSpec((None, t, d_b), lambda b, i: (b, 0, 0)),
                  pl.BlockSpec((None, d_b, t), lambda b, i: (b, 0, 0)),
                  pl.BlockSpec((None, t, LANES), lambda b, i: (b, 0, 0)),
                  pl.BlockSpec((None, tq, LANES), lambda b, i: (b, i, 0)),
                  pl.BlockSpec((None, tq, d_b), lambda b, i: (b, i, 0))],
        out_specs=pl.BlockSpec((None, tq, d_b), lambda b, i: (b, i, 0)),
        out_shape=jax.ShapeDtypeStruct((nb, t, d_b), BF16),
        compiler_params=_params(("parallel", "arbitrary")),
        name="fox_prompt",
    )(q3, k3, v3, k_aug, q_aug, z3)
    return out.reshape(nb * t, d_b)


def _fox_sample_body(pt_ref, q_ref, kn_ref, vn_ref, crow_ref, *refs, nh, dh, t_new, pages):
    del pt_ref
    k_refs = refs[0:pages]
    v_refs = refs[pages:2 * pages]
    lf_refs = refs[2 * pages:3 * pages]
    ob_ref = refs[3 * pages]
    qblk, m_scr, l_scr, acc_scr, carry_scr = refs[3 * pages + 1:]
    j = pl.program_id(1)
    d_b = nh * dh
    nrow = t_new * nh
    page = k_refs[0].shape[1]
    tp = crow_ref.shape[1]
    scale = dh ** -0.5
    head_mask = (_iota2((nh, d_b), 1) // dh) == _iota2((nh, d_b), 0)
    c_tiled = jnp.concatenate([crow_ref[...]] * t_new, axis=0)
    row_tok = _iota2((nrow, tp), 0) // nh
    lane_tok = _iota2((nrow, tp), 1)
    c_own = jnp.sum(jnp.where(lane_tok == row_tok, c_tiled, 0.0), axis=-1, keepdims=True)

    @pl.when(j == 0)
    def _():
        for t in range(t_new):
            qrow = jnp.broadcast_to(q_ref[t:t + 1, :].astype(F32), (nh, d_b))
            qblk[t * nh:(t + 1) * nh, :] = jnp.where(head_mask, qrow, 0.0)
        m_scr[...] = jnp.full(m_scr.shape, -jnp.inf, F32)
        l_scr[...] = jnp.zeros_like(l_scr)
        acc_scr[...] = jnp.zeros_like(acc_scr)
        carry_scr[...] = jnp.zeros_like(carry_scr)

    def update(s_tiles, v_blocks):
        tmax = [jnp.max(s, axis=-1, keepdims=True) for s in s_tiles]
        m_old = m_scr[...]
        m_new = functools.reduce(jnp.maximum, tmax, m_old)
        corr = jnp.exp(m_old - m_new)
        p_tiles = [jnp.exp(s - m_new) for s in s_tiles]
        psum = functools.reduce(lambda a, b: a + b, [jnp.sum(p, axis=-1, keepdims=True) for p in p_tiles])
        l_scr[...] = l_scr[...] * corr + psum
        pv = functools.reduce(lambda a, b: a + b, [_bdot_nt(p, v) for p, v in zip(p_tiles, v_blocks)])
        acc_scr[...] = acc_scr[...] * corr + pv
        m_scr[...] = m_new

    after = (_iota2((page, page), 0) > _iota2((page, page), 1)).astype(BF16)
    sums = jnp.concatenate([after, jnp.ones((page, page), BF16)], axis=1)
    lf = jnp.concatenate([lf_refs[g][...] for g in range(pages)], axis=0)
    cum = _xdot_r(lf, sums)
    run = carry_scr[...]
    biases = []
    for g in range(pages):
        biases.append(cum[g * nh:(g + 1) * nh, :page] + run)
        run = run + cum[g * nh:(g + 1) * nh, page:]
    carry_scr[...] = run

    qb = qblk[...].astype(BF16)
    raw = [jnp.dot(qb, k_refs[g][...].astype(BF16), preferred_element_type=F32) for g in range(pages)]
    s_tiles = [raw[g] * scale + jnp.concatenate([biases[g]] * t_new, axis=0) + c_own for g in range(pages)]
    update(s_tiles, [v_refs[g][...] for g in range(pages)])

    @pl.when(j == pl.num_programs(1) - 1)
    def _():
        s = jnp.dot(qb, kn_ref[...].astype(BF16), preferred_element_type=F32) * scale + c_own - c_tiled
        s = jnp.where(lane_tok <= row_tok, s, -jnp.inf)
        update([s], [vn_ref[...]])
        o = acc_scr[...] / l_scr[...]
        ob_ref[...] = jnp.zeros_like(ob_ref)
        for t in range(t_new):
            blk = jnp.where(head_mask, o[t * nh:(t + 1) * nh, :], 0.0)
            ob_ref[t:t + 1, :] = jnp.sum(blk, axis=0, keepdims=True)


def _fox_sample(qb, k3, v3, crow, cache_kt, cache_vt, cache_lft, page_table, layer, nb, tp, t_new, pages):
    d_b, page = cache_kt.shape[2:]
    nh = cache_lft.shape[2]
    dh = d_b // nh
    n_pages = page_table.shape[1]
    nj = n_pages // pages
    q3 = qb.reshape(nb, tp, d_b)

    def page_map(g):
        return lambda b, j, pt: (layer, pt[b, n_pages - 1 - (j * pages + g)], 0, 0)

    seq = lambda b, j, pt: (b, 0, 0)
    in_specs = [pl.BlockSpec((None, tp, d_b), seq),
                pl.BlockSpec((None, d_b, tp), seq),
                pl.BlockSpec((None, d_b, tp), seq),
                pl.BlockSpec((None, nh, tp), seq)]
    in_specs += [pl.BlockSpec((None, None, d_b, page), page_map(g)) for g in range(pages)]
    in_specs += [pl.BlockSpec((None, None, d_b, page), page_map(g)) for g in range(pages)]
    in_specs += [pl.BlockSpec((None, None, nh, page), page_map(g)) for g in range(pages)]
    nrow = t_new * nh
    grid_spec = pltpu.PrefetchScalarGridSpec(
        num_scalar_prefetch=1,
        grid=(nb, nj),
        in_specs=in_specs,
        out_specs=pl.BlockSpec((None, tp, d_b), seq),
        scratch_shapes=[pltpu.VMEM((nrow, d_b), F32),
                        pltpu.VMEM((nrow, 1), F32),
                        pltpu.VMEM((nrow, 1), F32),
                        pltpu.VMEM((nrow, d_b), F32),
                        pltpu.VMEM((nh, page), F32)],
    )
    out = pl.pallas_call(
        functools.partial(_fox_sample_body, nh=nh, dh=dh, t_new=t_new, pages=pages),
        grid_spec=grid_spec,
        out_shape=jax.ShapeDtypeStruct((nb, tp, d_b), F32),
        compiler_params=_params(("parallel", "arbitrary")),
        name="fox_sample",
    )(page_table, q3, k3, v3, crow, *([cache_kt] * pages), *([cache_vt] * pages), *([cache_lft] * pages))
    return out.reshape(nb * tp, d_b)


def _merge_body(x_ref, oa_ref, ob_ref, ga_ref, gb_ref, wpa_ref, wpb_ref, wo_ref, fg_ref, *rest, final_norm, gate_b):
    ob = ob_ref[...]
    if gate_b:
        zb_ref, y_ref = rest
        ob = ob * jax.nn.silu(zb_ref[...])
    else:
        (y_ref,) = rest
    ya = jnp.dot(oa_ref[...].astype(BF16), wpa_ref[...], preferred_element_type=F32)
    yb = jnp.dot(ob.astype(BF16), wpb_ref[...], preferred_element_type=F32)
    merged = jax.nn.sigmoid(ga_ref[...]) * ya + jax.nn.sigmoid(gb_ref[...]) * yb
    y = x_ref[...] + jnp.dot(merged.astype(BF16), wo_ref[...], preferred_element_type=F32)
    if final_norm:
        y = (y * lax.rsqrt(jnp.mean(y * y, axis=-1, keepdims=True) + EPS)) * fg_ref[...]
    y_ref[...] = y


def _merge(x2d, oa, ob, zb, ga, gb, wpa, wpb, wo, final_g, final_norm, tm):
    m, d = x2d.shape
    row = lambda w: pl.BlockSpec((tm, w), lambda i: (i, 0))
    const = lambda i: (0, 0)
    gate_b = zb is not None
    in_specs = [row(d), row(oa.shape[1]), row(ob.shape[1]), row(d), row(d),
                pl.BlockSpec(wpa.shape, const), pl.BlockSpec(wpb.shape, const), pl.BlockSpec(wo.shape, const),
                pl.BlockSpec((1, d), const)]
    args = [x2d, oa, ob, ga, gb, wpa, wpb, wo, final_g.reshape(1, d)]
    if gate_b:
        in_specs.append(row(zb.shape[1]))
        args.append(zb)
    return pl.pallas_call(
        functools.partial(_merge_body, final_norm=final_norm, gate_b=gate_b),
        grid=(m // tm,),
        in_specs=in_specs,
        out_specs=row(d),
        out_shape=jax.ShapeDtypeStruct((m, d), F32),
        compiler_params=_params(("parallel",)),
        name="merge",
    )(*args)


def _pack_in_weights(w_in_l, conv_dim, d_a, nh_a, d_b, nh_b, d_model):
    sizes = (conv_dim, d_a, nh_a, nh_a, 3 * d_b, nh_b, d_b, d_model, d_model)
    offs = [0]
    for s in sizes:
        offs.append(offs[-1] + s)
    col = lambda i: w_in_l[:, offs[i]:offs[i + 1]]
    qkv_b = col(4)
    wbig = jnp.concatenate([col(0), col(1), qkv_b[:, :d_b], col(6), col(7), col(8)], axis=1).astype(BF16)
    small = jnp.concatenate([col(5), col(2), col(3)], axis=1)
    wsmall = jnp.pad(small, ((0, 0), (0, LANES - small.shape[1]))).astype(BF16)
    wsmall_t = small.T.astype(BF16)
    wkv_t = qkv_b[:, d_b:].T.astype(BF16)
    return wbig, wsmall, wsmall_t, wkv_t


def _layer(x2d, nb, t, t_valid, conv0, s0, attend, lw, final_g, final_norm, tm, lb, prev_kv):
    (norm_g, w_in, conv_w, a_log, dt_bias, onorm_g, b_f, w_pa, w_pb, w_o) = lw
    d_model = x2d.shape[1]
    nh_a, dk, dv = s0.shape[1:]
    conv_dim = conv_w.shape[1]
    d_a = nh_a * dv
    nh_b = b_f.shape[0]
    d_b = w_pb.shape[0]
    wbig, wsmall, wsmall_t, wkv_t = _pack_in_weights(w_in, conv_dim, d_a, nh_a, d_b, nh_b, d_model)
    widths = (conv_dim, d_a, d_b, d_b, d_model, d_model)
    dtypes = (F32, F32, BF16, F32, F32, F32)
    qkva, za, qb, zb, ga, gb, small, logf_t, crow, k_aug, q_aug, kt, vt, vtb, kb, conv_tail = _inproj(
        x2d, norm_g, wbig, wsmall, wsmall_t, wkv_t, conv0, conv_w, b_f, widths, dtypes, tm, nb, t_valid, prev_kv)
    oa, s_fin = _delta(qkva, small, za, s0, a_log, dt_bias, onorm_g, nb, t, lb, t_valid, nh_b)
    ob, ob_gated = attend(qb, kb, kt, vt, vtb, crow, k_aug, q_aug, zb)
    y = _merge(x2d, oa, ob, None if ob_gated else zb, ga, gb, w_pa.astype(BF16), w_pb.astype(BF16),
               w_o.astype(BF16), final_g, final_norm, tm)
    return y, (kt, vt, logf_t, s_fin, conv_tail)


def kernel(x_prompt, x_sample, cache_k, cache_v, cache_logf, state_delta, state_conv, page_table, norm_g, w_in,
           conv_w, A_log, dt_bias, onorm_g, b_f, w_pa, w_pb, w_o, final_g):
    depth = norm_g.shape[0]
    nbp, tp_len, d_model = x_prompt.shape
    nbs, ts_len, _ = x_sample.shape
    nh_a, dk, dv = state_delta.shape[2:]
    conv_k, conv_dim = conv_w.shape[1:]
    nh_b, dh_b = cache_k.shape[3:]
    d_b = nh_b * dh_b
    ts_pad = LANES
    assert ts_len <= CHUNK and tp_len % LANES == 0 and conv_k - 1 <= SUBLANES and ts_len >= conv_k - 1

    hp = x_prompt.reshape(nbp * tp_len, d_model)
    hs = jnp.pad(x_sample, ((0, 0), (0, ts_pad - ts_len), (0, 0))).reshape(nbs * ts_pad, d_model)
    conv0_p = jnp.zeros((nbp, SUBLANES, conv_dim), F32)
    s0_p = jnp.zeros((nbp, nh_a, dk, dv), F32)
    n_pages = page_table.shape[1]
    page = cache_k.shape[2]
    assert dk == LANES and dv == LANES
    cache_kt = jnp.transpose(cache_k, (0, 1, 3, 4, 2)).reshape(depth, cache_k.shape[1], d_b, page)
    cache_vt = jnp.transpose(cache_v, (0, 1, 3, 4, 2)).reshape(depth, cache_v.shape[1], d_b, page)
    cache_lft = jnp.transpose(cache_logf, (0, 1, 3, 2))
    pages = next(g for g in (SAMPLE_PAGES_PER_STEP, 8, 4, 2, 1) if n_pages % g == 0)
    tq = 256 if tp_len % 256 == 0 else LANES
    lb_p = next(r for r in (512, 256, LANES) if tp_len % r == 0)
    tm_p = next(r for r in (512, 256, LANES) if tp_len % r == 0)
    tm_s = ts_pad

    st_p, st_s = [], []
    for l in range(depth):
        lw = (norm_g[l], w_in[l], conv_w[l], A_log[l], dt_bias[l], onorm_g[l], b_f[l], w_pa[l], w_pb[l], w_o[l])
        last = l == depth - 1

        def attend_p(qb, kb, kt, vt, vtb, crow, k_aug, q_aug, zb):
            del kt, vt, crow
            return _fox_prompt(qb, kb, vtb, k_aug, q_aug, zb, nb=nbp, t=tp_len, nh=nh_b, dh=dh_b, tq=tq), True

        prev_p = None if l == 0 else tuple(a if a.ndim == 4 else a[None] for a in st_p[-1][:2])
        hp, sp = _layer(hp, nbp, tp_len, None, conv0_p, s0_p, attend_p, lw, final_g, last, tm_p, lb_p, prev_p)

        conv0_s = jnp.pad(state_conv[l], ((0, 0), (SUBLANES - (conv_k - 1), 0), (0, 0)))

        def attend_s(qb, kb, kt, vt, vtb, crow, k_aug, q_aug, zb, l=l):
            del kb, vtb, k_aug, q_aug, zb
            return _fox_sample(qb, kt, vt, crow, cache_kt, cache_vt, cache_lft, page_table, l, nbs, ts_pad,
                               ts_len, pages), False

        hs, ss = _layer(hs, nbs, ts_pad, ts_len, conv0_s, state_delta[l], attend_s, lw, final_g, last,
                        tm_s, ts_pad, None)
        st_p.append(sp)
        st_s.append(ss)

    def seq_view(a, nb, t, keep):
        return a.reshape(nb, t, *a.shape[1:])[:, :keep]

    def token_major(feat_major, nb, t, keep):
        a = feat_major.reshape(depth, nb, nh_b, dh_b, t)[..., :keep]
        return jnp.transpose(a, (0, 1, 4, 2, 3))

    y_prompt = hp.reshape(nbp, tp_len, d_model)
    y_sample = seq_view(hs, nbs, ts_pad, ts_len)
    outs_p, outs_s = [], []
    for (st, nb, t, keep, outs) in ((st_p, nbp, tp_len, tp_len, outs_p), (st_s, nbs, ts_pad, ts_len, outs_s)):
        gathered = st[-1][0].ndim == 4 and st[-1][0].shape[0] == depth
        k_all = token_major(st[-1][0] if gathered else jnp.stack([s[0] for s in st]), nb, t, keep)
        v_all = token_major(st[-1][1] if gathered else jnp.stack([s[1] for s in st]), nb, t, keep)
        logf_all = jnp.transpose(jnp.stack([s[2] for s in st])[..., :keep], (0, 1, 3, 2))
        delta_all = jnp.stack([s[3] for s in st])
        conv_all = jnp.stack([s[4][:, SUBLANES - (conv_k - 1):] for s in st])
        outs.extend([k_all, v_all, logf_all, delta_all, conv_all])
    return (y_prompt, y_sample, *outs_p, *outs_s)
```

```python
import functools
import math

import jax
import jax.numpy as jnp
from jax import lax
from jax.experimental import pallas as pl
from jax.experimental.pallas import tpu as pltpu

F32 = jnp.float32
BF16 = jnp.bfloat16
EPS = 1e-6
CHUNK = 64
LANES = 128
SUBLANES = 8
BF16_SUBLANES = 16
SMALL_COLS = 16
VMEM_LIMIT = 56 * 1024 * 1024
FOX_HEADS_PER_PASS = 8
FOX_TK = 256
SAMPLE_PAGES_PER_STEP = 32
MERGE_ROWS = 1024
CONV_COLS = 256
BIAS_LANES = 6

_NT = (((1,), (1,)), ((), ()))
_TN = (((0,), (0,)), ((), ()))


def _bdot(a, b):
    return jnp.dot(a.astype(BF16), b.astype(BF16), preferred_element_type=F32)


def _bdot_nt(a, b):
    return lax.dot_general(a.astype(BF16), b.astype(BF16), _NT, preferred_element_type=F32)


def _bdot_tn(a, b):
    return lax.dot_general(a.astype(BF16), b.astype(BF16), _TN, preferred_element_type=F32)


def _split3(x):
    x1 = x.astype(BF16)
    r1 = x - x1.astype(F32)
    x2 = r1.astype(BF16)
    x3 = (r1 - x2.astype(F32)).astype(BF16)
    return x1, x2, x3


def _xdot(mask_bf16, x, dims=None):
    parts = _split3(x)
    if dims is None:
        return sum(jnp.dot(mask_bf16, p, preferred_element_type=F32) for p in parts)
    return sum(lax.dot_general(mask_bf16, p, dims, preferred_element_type=F32) for p in parts)


def _xdot_r(x, mask_bf16, dims=None):
    parts = _split3(x)
    if dims is None:
        return sum(jnp.dot(p, mask_bf16, preferred_element_type=F32) for p in parts)
    return sum(lax.dot_general(p, mask_bf16, dims, preferred_element_type=F32) for p in parts)


def _iota2(shape, dim):
    return lax.broadcasted_iota(jnp.int32, shape, dim)


def _params(sem):
    return pltpu.CompilerParams(dimension_semantics=sem, vmem_limit_bytes=VMEM_LIMIT)


def _inproj_body(x_ref, g_ref, w_ref, ws_ref, wst_ref, wkvt_ref, conv0_ref, cw_ref, bfr_ref, bfc_ref, *refs,
                 segs, nj, t_last, nh, n_prev):
    prev_refs, refs = (refs[:2], refs[2:]) if n_prev else ((), refs)
    out_refs, (xbuf, carry_c, carry_r) = refs[:-3], refs[-3:]
    i = pl.program_id(0)
    tm = x_ref.shape[0]
    pad = SUBLANES
    conv_k = cw_ref.shape[0]
    x = x_ref[...]
    h = (x * lax.rsqrt(jnp.mean(x * x, axis=-1, keepdims=True) + EPS)) * g_ref[...]
    hb = h.astype(BF16)

    @pl.when(i % nj == 0)
    def _():
        xbuf[0:pad, :] = conv0_ref[...]

    @pl.when(i % nj != 0)
    def _():
        xbuf[0:pad, :] = xbuf[tm:tm + pad, :]

    n_tail = 10
    plain = [(o_ref, lo, c0, min(CONV_COLS, hi - lo - c0))
             for (lo, hi), o_ref in zip(segs[1:], out_refs[1:-n_tail]) for c0 in range(0, hi - lo, CONV_COLS)]

    def project(o_ref, lo, c0, width):
        o_ref[:, c0:c0 + width] = jnp.dot(hb, w_ref[:, lo + c0:lo + c0 + width],
                                          preferred_element_type=F32).astype(o_ref.dtype)

    (lo, hi) = segs[0]
    n_conv = (hi - lo) // CONV_COLS
    per_conv = -(-len(plain) // n_conv)
    for blk in range(n_conv):
        c0 = blk * CONV_COLS
        cols = slice(c0, c0 + CONV_COLS)
        xbuf[pad:pad + tm, cols] = jnp.dot(hb, w_ref[:, lo + c0:lo + c0 + CONV_COLS], preferred_element_type=F32)
        for task in plain[blk * per_conv:(blk + 1) * per_conv]:
            project(*task)
        acc = None
        for tap in range(conv_k):
            start = pad - (conv_k - 1) + tap
            term = cw_ref[tap:tap + 1, cols] * xbuf[start:start + tm, cols]
            acc = term if acc is None else acc + term
        out_refs[0][:, cols] = jax.nn.silu(acc)
    sm_ref, logf_ref, crow_ref, kaug_ref, qaug_ref, kt_ref, vt_ref, vtb_ref, kb_ref, cst_ref = out_refs[-n_tail:]
    small = jnp.dot(hb, ws_ref[...], preferred_element_type=F32)
    small_t = lax.dot_general(wst_ref[...], hb, _NT, preferred_element_type=F32)
    sm_ref[...] = small

    @pl.when(i % nj == 0)
    def _():
        carry_c[...] = jnp.zeros_like(carry_c)
        carry_r[...] = jnp.zeros_like(carry_r)

    logf_c = jax.nn.log_sigmoid(small + bfr_ref[...])
    tril = (_iota2((tm, tm), 0) >= _iota2((tm, tm), 1)).astype(BF16)
    ccol = _xdot(tril, logf_c) + carry_c[...]
    carry_c[...] = ccol[tm - 1:tm, :]
    li = _iota2((LANES, LANES), 0)
    lj = _iota2((LANES, LANES), 1)
    lane = _iota2((1, LANES), 1)
    in_heads = lane < nh * BIAS_LANES
    half = BIAS_LANES // 2
    k_aug = jnp.where(in_heads & (lane % BIAS_LANES >= half), 1.0, 0.0)
    q_aug = jnp.where(in_heads & (lane % BIAS_LANES < half), 1.0, 0.0)
    for p, piece in enumerate(_split3(ccol)):
        to_k = ((lj == BIAS_LANES * li + p) & (li < nh)).astype(BF16)
        to_q = ((lj == BIAS_LANES * li + half + p) & (li < nh)).astype(BF16)
        k_aug = k_aug - jnp.dot(piece, to_k, preferred_element_type=F32)
        q_aug = q_aug + jnp.dot(piece, to_q, preferred_element_type=F32)
    kaug_ref[...] = k_aug.astype(BF16)
    qaug_ref[...] = q_aug.astype(BF16)
    logf_r = jax.nn.log_sigmoid(small_t[0:nh, :] + bfc_ref[...])
    logf_ref[...] = logf_r
    triu = (_iota2((tm, tm), 0) <= _iota2((tm, tm), 1)).astype(BF16)
    crow = _xdot_r(logf_r, triu) + carry_r[:, 0:1]
    crow_ref[...] = crow
    carry_r[...] = jnp.broadcast_to(crow[:, tm - 1:tm], carry_r.shape)

    d_b = vtb_ref.shape[0]
    kvt = lax.dot_general(wkvt_ref[...], hb, _NT, preferred_element_type=F32)
    if n_prev:
        kt_ref[0:n_prev] = prev_refs[0][...]
        vt_ref[0:n_prev] = prev_refs[1][...]
        kt_ref[n_prev] = kvt[:d_b]
        vt_ref[n_prev] = kvt[d_b:]
    else:
        kt_ref[...] = kvt[:d_b]
        vt_ref[...] = kvt[d_b:]
    vtb_ref[...] = kvt[d_b:].astype(BF16)
    kb_ref[...] = kvt[:d_b].T.astype(BF16)
    cst_ref[...] = xbuf[t_last:t_last + pad, :]


def _inproj(x2d, g, wbig, wsmall, wsmall_t, wkv_t, conv0, conv_w, b_f, widths, dtypes, tm, nb, t_valid, prev_kv):
    m, d = x2d.shape
    t = m // nb
    nj = t // tm
    d_b = wkv_t.shape[0] // 2
    conv_dim = conv_w.shape[1]
    nh = b_f.shape[0]
    bfr = jnp.zeros((1, LANES), F32).at[0, :nh].set(b_f)
    bfc = b_f.reshape(nh, 1)
    t_last = (t_valid if t_valid is not None else t) - (nj - 1) * tm
    assert 0 < t_last <= tm and (t_valid is None or nj == 1)
    offs = [0]
    for w in widths:
        offs.append(offs[-1] + w)
    segs = tuple((offs[i], offs[i + 1]) for i in range(len(widths)))
    const = lambda i: (0, 0)
    seq_t = pl.BlockSpec((None, d_b, tm), lambda i: (i // nj, 0, i % nj))
    seq_h = pl.BlockSpec((None, nh, tm), lambda i: (i // nj, 0, i % nj))
    seq_l = pl.BlockSpec((None, tm, LANES), lambda i: (i // nj, i % nj, 0))
    seq_c = pl.BlockSpec((None, SUBLANES, conv_dim), lambda i: (i // nj, 0, 0))
    n_prev = 0 if prev_kv is None else prev_kv[0].shape[0]
    if n_prev:
        kv_shape = jax.ShapeDtypeStruct((n_prev + 1, nb, d_b, t), F32)
        kv_spec = pl.BlockSpec((n_prev + 1, None, d_b, tm), lambda i: (0, i // nj, 0, i % nj))
        prev_spec = pl.BlockSpec((n_prev, None, d_b, tm), lambda i: (0, i // nj, 0, i % nj))
    else:
        kv_shape, kv_spec = jax.ShapeDtypeStruct((nb, d_b, t), F32), seq_t
    out_shape = [jax.ShapeDtypeStruct((m, w), dt) for w, dt in zip(widths, dtypes)]
    out_shape += [jax.ShapeDtypeStruct((m, LANES), F32),
                  jax.ShapeDtypeStruct((nb, nh, t), F32), jax.ShapeDtypeStruct((nb, nh, t), F32),
                  jax.ShapeDtypeStruct((nb, t, LANES), BF16), jax.ShapeDtypeStruct((nb, t, LANES), BF16),
                  kv_shape, kv_shape,
                  jax.ShapeDtypeStruct((nb, d_b, t), BF16), jax.ShapeDtypeStruct((m, d_b), BF16),
                  jax.ShapeDtypeStruct((nb, SUBLANES, conv_dim), F32)]
    out_specs = [pl.BlockSpec((tm, w), lambda i: (i, 0)) for w in widths]
    out_specs += [pl.BlockSpec((tm, LANES), lambda i: (i, 0)), seq_h, seq_h, seq_l, seq_l,
                  kv_spec, kv_spec, seq_t, pl.BlockSpec((tm, d_b), lambda i: (i, 0)), seq_c]
    in_specs = [pl.BlockSpec((tm, d), lambda i: (i, 0)),
                pl.BlockSpec((1, d), const),
                pl.BlockSpec(wbig.shape, const, pipeline_mode=pl.Buffered(1)),
                pl.BlockSpec(wsmall.shape, const, pipeline_mode=pl.Buffered(1)),
                pl.BlockSpec(wsmall_t.shape, const, pipeline_mode=pl.Buffered(1)),
                pl.BlockSpec(wkv_t.shape, const, pipeline_mode=pl.Buffered(1)),
                seq_c,
                pl.BlockSpec(conv_w.shape, const),
                pl.BlockSpec((1, LANES), const),
                pl.BlockSpec((nh, 1), const)]
    args = [x2d, g.reshape(1, d), wbig, wsmall, wsmall_t, wkv_t, conv0, conv_w, bfr, bfc]
    if n_prev:
        in_specs += [prev_spec, prev_spec]
        args += list(prev_kv)
    return pl.pallas_call(
        functools.partial(_inproj_body, segs=segs, nj=nj, t_last=t_last, nh=nh, n_prev=n_prev),
        grid=(m // tm,),
        in_specs=in_specs,
        out_specs=out_specs,
        out_shape=out_shape,
        scratch_shapes=[pltpu.VMEM((tm + 2 * SUBLANES, conv_dim), F32),
                        pltpu.VMEM((1, LANES), F32),
                        pltpu.VMEM((nh, LANES), F32)],
        compiler_params=_params(("arbitrary",)),
        name="inproj",
    )(*args)


def _delta_body(qkv_ref, sm_ref, za_ref, s0_ref, alog_ref, dtb_ref, og_ref,
                oa_ref, sfin_ref, s_scr, *, nh, dk, dv, t_valid, n_chunks, beta_lane):
    j = pl.program_id(1)
    lb = qkv_ref.shape[0]
    c = CHUNK

    @pl.when(j == 0)
    def _():
        s_scr[...] = s0_ref[...]

    y = qkv_ref[...]

    sm = sm_ref[...]
    beta_all = jax.nn.sigmoid(sm)
    g_all = -jnp.exp(alog_ref[...]) * jax.nn.softplus(sm + dtb_ref[...])
    if t_valid is not None:
        tok = j * lb + _iota2((lb, LANES), 0)
        valid = tok < t_valid
        beta_all = jnp.where(valid, beta_all, 0.0)
        g_all = jnp.where(valid, g_all, 0.0)

    ri = _iota2((c, c), 0)
    ci = _iota2((c, c), 1)
    tril_b = (ri >= ci).astype(BF16)
    lower = ri >= ci
    strict = ri > ci
    eye = (ri == ci).astype(F32)
    og = og_ref[...]
    qscale = dk ** -0.5
    kbase = nh * dk
    vbase = 2 * nh * dk
    a_lane = beta_lane + nh

    probs = [(ch, h) for ch in range(n_chunks) for h in range(nh)]
    strict_w = _iota2((c, LANES), 0) > _iota2((c, LANES), 1)
    qs, ks, vs, betas = {}, {}, {}, {}
    diffs, gcums = {}, {}
    for ch in range(n_chunks):
        r0 = ch * c
        cols = []
        for h in range(nh):
            q = y[r0:r0 + c, h * dk:(h + 1) * dk]
            k = y[r0:r0 + c, kbase + h * dk:kbase + (h + 1) * dk]
            qs[ch, h] = q * lax.rsqrt(jnp.sum(q * q, axis=-1, keepdims=True) + EPS) * qscale
            ks[ch, h] = k * lax.rsqrt(jnp.sum(k * k, axis=-1, keepdims=True) + EPS)
            vs[ch, h] = y[r0:r0 + c, vbase + h * dv:vbase + (h + 1) * dv]
            betas[ch, h] = beta_all[r0:r0 + c, beta_lane + h:beta_lane + h + 1]
            g = g_all[r0:r0 + c, a_lane + h:a_lane + h + 1]
            g_w = jnp.broadcast_to(g, (c, LANES))
            cols += [jnp.where(strict_w, g_w, 0.0), g_w]
        cum = _xdot(tril_b, jnp.concatenate(cols, axis=1))
        for h in range(nh):
            diffs[ch, h] = cum[:, 2 * h * LANES:2 * h * LANES + c]
            gcums[ch, h] = cum[:, (2 * h + 1) * LANES:(2 * h + 2) * LANES]

    decays, egcs, kbs, n_pows, t_invs = {}, {}, {}, {}, {}
    for p in probs:
        decays[p] = jnp.where(lower, jnp.exp(diffs[p]), 0.0)
        egcs[p] = jnp.exp(gcums[p])
        kbs[p] = ks[p] * betas[p]
    for p in probs:
        n_pows[p] = -jnp.where(strict, _bdot_nt(kbs[p], ks[p]) * decays[p], 0.0)
        t_invs[p] = eye + n_pows[p]
    span = 2
    while span < c:
        for p in probs:
            n_pows[p] = _bdot(n_pows[p], n_pows[p])
        for p in probs:
            t_invs[p] = t_invs[p] + _bdot(t_invs[p], n_pows[p])
        span *= 2
    us, ws, attns = {}, {}, {}
    for p in probs:
        uw = _bdot(t_invs[p], jnp.concatenate([vs[p] * betas[p], kbs[p] * egcs[p]], axis=1))
        us[p] = uw[:, :dv]
        ws[p] = uw[:, dv:]
    for p in probs:
        attns[p] = _bdot_nt(qs[p], ks[p]) * decays[p]

    for ch in range(n_chunks):
        r0 = ch * c
        s_olds = [s_scr[h] for h in range(nh)]
        wss = [_bdot(jnp.concatenate([ws[ch, h], qs[ch, h] * egcs[ch, h]], axis=0), s_olds[h]) for h in range(nh)]
        v_news = [us[ch, h] - wss[h][:c] for h in range(nh)]
        os_ = [wss[h][c:] + _bdot(attns[ch, h], v_news[h]) for h in range(nh)]
        for h in range(nh):
            gcum = gcums[ch, h]
            glast = gcum[c - 1:c, :]
            kd = ks[ch, h] * jnp.exp(glast - gcum)
            s_scr[h] = s_olds[h] * jnp.exp(glast[:, :dv]) + _bdot_tn(kd, v_news[h])
        for h in range(nh):
            o = os_[h]
            o = o * lax.rsqrt(jnp.mean(o * o, axis=-1, keepdims=True) + EPS) * og
            z = za_ref[r0:r0 + c, h * dv:(h + 1) * dv]
            oa_ref[r0:r0 + c, h * dv:(h + 1) * dv] = (o * jax.nn.silu(z)).astype(oa_ref.dtype)

    @pl.when(j == pl.num_programs(1) - 1)
    def _():
        sfin_ref[...] = s_scr[...]


def _delta(qkva, small, za, s0, a_log, dt_bias, onorm_g, nb, t, lb, t_valid, beta_lane):
    nh, dk, dv = s0.shape[1:]
    conv_dim = qkva.shape[1]
    nj = t // lb
    a_lane = beta_lane + nh
    alog_row = jnp.zeros((1, LANES), F32).at[0, a_lane:a_lane + nh].set(a_log)
    dtb_row = jnp.zeros((1, LANES), F32).at[0, a_lane:a_lane + nh].set(dt_bias)
    const = lambda b, j: (0, 0)
    body = functools.partial(_delta_body, nh=nh, dk=dk, dv=dv, t_valid=t_valid,
                             n_chunks=lb // CHUNK, beta_lane=beta_lane)
    return pl.pallas_call(
        body,
        grid=(nb, nj),
        in_specs=[pl.BlockSpec((lb, conv_dim), lambda b, j: (b * nj + j, 0)),
                  pl.BlockSpec((lb, LANES), lambda b, j: (b * nj + j, 0)),
                  pl.BlockSpec((lb, nh * dv), lambda b, j: (b * nj + j, 0)),
                  pl.BlockSpec((None, nh, dk, dv), lambda b, j: (b, 0, 0, 0)),
                  pl.BlockSpec((1, LANES), const),
                  pl.BlockSpec((1, LANES), const),
                  pl.BlockSpec((1, dv), const)],
        out_specs=[pl.BlockSpec((lb, nh * dv), lambda b, j: (b * nj + j, 0)),
                   pl.BlockSpec((None, nh, dk, dv), lambda b, j: (b, 0, 0, 0))],
        out_shape=[jax.ShapeDtypeStruct((nb * t, nh * dv), BF16),
                   jax.ShapeDtypeStruct((nb, nh, dk, dv), F32)],
        scratch_shapes=[pltpu.VMEM((nh, dk, dv), F32)],
        compiler_params=_params(("parallel", "arbitrary")),
        name="delta",
    )(qkva, small, za, s0, alog_row, dtb_row, onorm_g.reshape(1, dv))


def _is_pow2(x):
    return math.frexp(x)[0] == 0.5


def _fox_body(q_ref, k_ref, v_ref, kaug_ref, qaug_ref, zb_ref, ob_ref, *, nh, dh, tk, heads_per_pass):
    qi = pl.program_id(1)
    tq = q_ref.shape[0]
    scale = dh ** -0.5
    lane = _iota2((1, LANES), 1)
    frow = _iota2((LANES, 1), 0)
    per = LANES // dh
    krow = _iota2((tk, tq), 0)
    qcol = _iota2((tk, tq), 1)
    ones_rows = jnp.ones((BF16_SUBLANES, tk), BF16)
    q_aug = qaug_ref[...]

    for h0 in range(0, nh, heads_per_pass):
        heads = list(range(h0, h0 + heads_per_pass))
        gls, halves, qhs = [], [], []
        for h in heads:
            grp, hh = divmod(h, per)
            gl = slice(grp * LANES, (grp + 1) * LANES)
            in_head = (lane >= hh * dh) & (lane < (hh + 1) * dh)
            q2 = q_ref[:, gl] * jnp.asarray(scale, q_ref.dtype)
            gls.append(gl)
            halves.append(hh)
            qh = jnp.where(in_head, q2, jnp.zeros_like(q2))
            own_bias = (lane >= h * BIAS_LANES) & (lane < (h + 1) * BIAS_LANES)
            qb = jnp.where(own_bias, q_aug, jnp.zeros_like(q_aug))
            qhs.append(jnp.concatenate([qh, qb], axis=1))

        def step(jb, carry, diag=None):
            masked = diag is not None
            ks = pl.multiple_of(jb * tk, tk)
            k_aug = kaug_ref[pl.ds(ks, tk), :]
            ss = []
            for i, h in enumerate(heads):
                k2 = k_ref[pl.ds(ks, tk), gls[i]]
                ss.append(_bdot_nt(jnp.concatenate([k2, k_aug], axis=1), qhs[i]))
            ps, new = [], []
            for i, h in enumerate(heads):
                m, acc = carry[i]
                s = ss[i]
                if masked:
                    s = jnp.where(krow + diag * tk <= qcol, s, -jnp.inf)
                m_new = jnp.maximum(m, jnp.max(s, axis=0, keepdims=True))
                ps.append(jnp.exp(s - m_new).astype(BF16))
                new.append((m_new, acc * jnp.exp(m - m_new)))
            out = []
            for i, h in enumerate(heads):
                m_new, acc = new[i]
                v_ext = jnp.concatenate([v_ref[gls[i], pl.ds(ks, tk)], ones_rows], axis=0)
                out.append((m_new, acc + jnp.dot(v_ext, ps[i], preferred_element_type=F32)))
            return tuple(out)

        init = tuple((jnp.full((1, tq), -jnp.inf, F32), jnp.zeros((LANES + BF16_SUBLANES, tq), F32))
                     for _ in heads)
        per_q = tq // tk
        fin = lax.fori_loop(0, qi * per_q, step, init)
        for d in range(per_q):
            fin = step(qi * per_q + d, fin, diag=d)
        for g0 in range(0, heads_per_pass, per):
            o2 = None
            for i in range(g0, g0 + per):
                _, acc = fin[i]
                o = acc[:LANES] / acc[LANES:LANES + 1]
                keep = (frow >= halves[i] * dh) & (frow < (halves[i] + 1) * dh)
                o2 = o if o2 is None else jnp.where(keep, o, o2)
            gl = gls[g0]
            ob_ref[:, gl] = (o2.T * jax.nn.silu(zb_ref[:, gl])).astype(ob_ref.dtype)


def _fox_prompt(qb, kb, v3, k_aug, q_aug, zb, nb, t, nh, dh, tq):
    d_b = nh * dh
    nq = t // tq
    q3 = qb.reshape(nb, t, d_b)
    k3 = kb.reshape(nb, t, d_b)
    z3 = zb.reshape(nb, t, d_b)
    assert nh * BIAS_LANES <= LANES and v3.dtype == BF16 and _is_pow2(dh ** -0.5)
    out = pl.pallas_call(
        functools.partial(_fox_body, nh=nh, dh=dh, tk=min(FOX_TK, tq), heads_per_pass=FOX_HEADS_PER_PASS),
        grid=(nb, nq),
        in_specs=[pl.BlockSpec((None, tq, d_b), lambda b, i: (b, i, 0)),
                  pl.BlockSpec((None, t, d_b), lambda b, i: (b, 0, 0)),
                  pl.BlockSpec((None, d_b, t), lambda b, i: (b, 0, 0)),
                  pl.BlockSpec((None, t, LANES), lambda b, i: (b, 0, 0)),
                  pl.BlockSpec((None, tq, LANES), lambda b, i: (b, i, 0)),
                  pl.BlockSpec((None, tq, d_b), lambda b, i: (b, i, 0))],
        out_specs=pl.BlockSpec((None, tq, d_b), lambda b, i: (b, i, 0)),
        out_shape=jax.ShapeDtypeStruct((nb, t, d_b), BF16),
        compiler_params=_params(("parallel", "arbitrary")),
        name="fox_prompt",
    )(q3, k3, v3, k_aug, q_aug, z3)
    return out.reshape(nb * t, d_b)


def _fox_sample_body(pt_ref, q_ref, kn_ref, vn_ref, crow_ref, *refs, nh, dh, t_new, pages):
    del pt_ref
    k_refs = refs[0:pages]
    v_refs = refs[pages:2 * pages]
    lf_refs = refs[2 * pages:3 * pages]
    ob_ref = refs[3 * pages]
    qblk, m_scr, l_scr, acc_scr, carry_scr = refs[3 * pages + 1:]
    j = pl.program_id(1)
    d_b = nh * dh
    nrow = t_new * nh
    page = k_refs[0].shape[1]
    tp = crow_ref.shape[1]
    scale = dh ** -0.5
    head_mask = (_iota2((nh, d_b), 1) // dh) == _iota2((nh, d_b), 0)
    c_tiled = jnp.concatenate([crow_ref[...]] * t_new, axis=0)
    row_tok = _iota2((nrow, tp), 0) // nh
    lane_tok = _iota2((nrow, tp), 1)
    c_own = jnp.sum(jnp.where(lane_tok == row_tok, c_tiled, 0.0), axis=-1, keepdims=True)

    @pl.when(j == 0)
    def _():
        for t in range(t_new):
            qrow = jnp.broadcast_to(q_ref[t:t + 1, :].astype(F32), (nh, d_b))
            qblk[t * nh:(t + 1) * nh, :] = jnp.where(head_mask, qrow, 0.0)
        m_scr[...] = jnp.full(m_scr.shape, -jnp.inf, F32)
        l_scr[...] = jnp.zeros_like(l_scr)
        acc_scr[...] = jnp.zeros_like(acc_scr)
        carry_scr[...] = jnp.zeros_like(carry_scr)

    def update(s_tiles, v_blocks):
        tmax = [jnp.max(s, axis=-1, keepdims=True) for s in s_tiles]
        m_old = m_scr[...]
        m_new = functools.reduce(jnp.maximum, tmax, m_old)
        corr = jnp.exp(m_old - m_new)
        p_tiles = [jnp.exp(s - m_new) for s in s_tiles]
        psum = functools.reduce(lambda a, b: a + b, [jnp.sum(p, axis=-1, keepdims=True) for p in p_tiles])
        l_scr[...] = l_scr[...] * corr + psum
        pv = functools.reduce(lambda a, b: a + b, [_bdot_nt(p, v) for p, v in zip(p_tiles, v_blocks)])
        acc_scr[...] = acc_scr[...] * corr + pv
        m_scr[...] = m_new

    after = (_iota2((page, page), 0) > _iota2((page, page), 1)).astype(BF16)
    sums = jnp.concatenate([after, jnp.ones((page, page), BF16)], axis=1)
    lf = jnp.concatenate([lf_refs[g][...] for g in range(pages)], axis=0)
    cum = _xdot_r(lf, sums)
    run = carry_scr[...]
    biases = []
    for g in range(pages):
        biases.append(cum[g * nh:(g + 1) * nh, :page] + run)
        run = run + cum[g * nh:(g + 1) * nh, page:]
    carry_scr[...] = run

    qb = qblk[...].astype(BF16)
    raw = [jnp.dot(qb, k_refs[g][...].astype(BF16), preferred_element_type=F32) for g in range(pages)]
    s_tiles = [raw[g] * scale + jnp.concatenate([biases[g]] * t_new, axis=0) + c_own for g in range(pages)]
    update(s_tiles, [v_refs[g][...] for g in range(pages)])

    @pl.when(j == pl.num_programs(1) - 1)
    def _():
        s = jnp.dot(qb, kn_ref[...].astype(BF16), preferred_element_type=F32) * scale + c_own - c_tiled
        s = jnp.where(lane_tok <= row_tok, s, -jnp.inf)
        update([s], [vn_ref[...]])
        o = acc_scr[...] / l_scr[...]
        ob_ref[...] = jnp.zeros_like(ob_ref)
        for t in range(t_new):
            blk = jnp.where(head_mask, o[t * nh:(t + 1) * nh, :], 0.0)
            ob_ref[t:t + 1, :] = jnp.sum(blk, axis=0, keepdims=True)


def _fox_sample(qb, k3, v3, crow, cache_kt, cache_vt, cache_lft, page_table, layer, nb, tp, t_new, pages):
    d_b, page = cache_kt.shape[2:]
    nh = cache_lft.shape[2]
    dh = d_b // nh
    n_pages = page_table.shape[1]
    nj = n_pages // pages
    q3 = qb.reshape(nb, tp, d_b)

    def page_map(g):
        return lambda b, j, pt: (layer, pt[b, n_pages - 1 - (j * pages + g)], 0, 0)

    seq = lambda b, j, pt: (b, 0, 0)
    in_specs = [pl.BlockSpec((None, tp, d_b), seq),
                pl.BlockSpec((None, d_b, tp), seq),
                pl.BlockSpec((None, d_b, tp), seq),
                pl.BlockSpec((None, nh, tp), seq)]
    in_specs += [pl.BlockSpec((None, None, d_b, page), page_map(g)) for g in range(pages)]
    in_specs += [pl.BlockSpec((None, None, d_b, page), page_map(g)) for g in range(pages)]
    in_specs += [pl.BlockSpec((None, None, nh, page), page_map(g)) for g in range(pages)]
    nrow = t_new * nh
    grid_spec = pltpu.PrefetchScalarGridSpec(
        num_scalar_prefetch=1,
        grid=(nb, nj),
        in_specs=in_specs,
        out_specs=pl.BlockSpec((None, tp, d_b), seq),
        scratch_shapes=[pltpu.VMEM((nrow, d_b), F32),
                        pltpu.VMEM((nrow, 1), F32),
                        pltpu.VMEM((nrow, 1), F32),
                        pltpu.VMEM((nrow, d_b), F32),
                        pltpu.VMEM((nh, page), F32)],
    )
    out = pl.pallas_call(
        functools.partial(_fox_sample_body, nh=nh, dh=dh, t_new=t_new, pages=pages),
        grid_spec=grid_spec,
        out_shape=jax.ShapeDtypeStruct((nb, tp, d_b), F32),
        compiler_params=_params(("parallel", "arbitrary")),
        name="fox_sample",
    )(page_table, q3, k3, v3, crow, *([cache_kt] * pages), *([cache_vt] * pages), *([cache_lft] * pages))
    return out.reshape(nb * tp, d_b)


def _merge_body(x_ref, oa_ref, ob_ref, ga_ref, gb_ref, wpa_ref, wpb_ref, wo_ref, fg_ref, *rest, final_norm, gate_b):
    ob = ob_ref[...]
    if gate_b:
        zb_ref, y_ref = rest
        ob = ob * jax.nn.silu(zb_ref[...])
    else:
        (y_ref,) = rest
    ya = jnp.dot(oa_ref[...].astype(BF16), wpa_ref[...], preferred_element_type=F32)
    yb = jnp.dot(ob.astype(BF16), wpb_ref[...], preferred_element_type=F32)
    merged = jax.nn.sigmoid(ga_ref[...]) * ya + jax.nn.sigmoid(gb_ref[...]) * yb
    y = x_ref[...] + jnp.dot(merged.astype(BF16), wo_ref[...], preferred_element_type=F32)
    if final_norm:
        y = (y * lax.rsqrt(jnp.mean(y * y, axis=-1, keepdims=True) + EPS)) * fg_ref[...]
    y_ref[...] = y


def _merge(x2d, oa, ob, zb, ga, gb, wpa, wpb, wo, final_g, final_norm, tm):
    m, d = x2d.shape
    row = lambda w: pl.BlockSpec((tm, w), lambda i: (i, 0))
    const = lambda i: (0, 0)
    gate_b = zb is not None
    in_specs = [row(d), row(oa.shape[1]), row(ob.shape[1]), row(d), row(d),
                pl.BlockSpec(wpa.shape, const), pl.BlockSpec(wpb.shape, const), pl.BlockSpec(wo.shape, const),
                pl.BlockSpec((1, d), const)]
    args = [x2d, oa, ob, ga, gb, wpa, wpb, wo, final_g.reshape(1, d)]
    if gate_b:
        in_specs.append(row(zb.shape[1]))
        args.append(zb)
    return pl.pallas_call(
        functools.partial(_merge_body, final_norm=final_norm, gate_b=gate_b),
        grid=(m // tm,),
        in_specs=in_specs,
        out_specs=row(d),
        out_shape=jax.ShapeDtypeStruct((m, d), F32),
        compiler_params=_params(("parallel",)),
        name="merge",
    )(*args)


def _pack_in_weights(w_in_l, conv_dim, d_a, nh_a, d_b, nh_b, d_model):
    sizes = (conv_dim, d_a, nh_a, nh_a, 3 * d_b, nh_b, d_b, d_model, d_model)
    offs = [0]
    for s in sizes:
        offs.append(offs[-1] + s)
    col = lambda i: w_in_l[:, offs[i]:offs[i + 1]]
    qkv_b = col(4)
    wbig = jnp.concatenate([col(0), col(1), qkv_b[:, :d_b], col(6), col(7), col(8)], axis=1).astype(BF16)
    small = jnp.concatenate([col(5), col(2), col(3)], axis=1)
    wsmall = jnp.pad(small, ((0, 0), (0, LANES - small.shape[1]))).astype(BF16)
    wsmall_t = small.T.astype(BF16)
    wkv_t = qkv_b[:, d_b:].T.astype(BF16)
    return wbig, wsmall, wsmall_t, wkv_t


def _layer(x2d, nb, t, t_valid, conv0, s0, attend, lw, final_g, final_norm, tm, lb, prev_kv):
    (norm_g, w_in, conv_w, a_log, dt_bias, onorm_g, b_f, w_pa, w_pb, w_o) = lw
    d_model = x2d.shape[1]
    nh_a, dk, dv = s0.shape[1:]
    conv_dim = conv_w.shape[1]
    d_a = nh_a * dv
    nh_b = b_f.shape[0]
    d_b = w_pb.shape[0]
    wbig, wsmall, wsmall_t, wkv_t = _pack_in_weights(w_in, conv_dim, d_a, nh_a, d_b, nh_b, d_model)
    widths = (conv_dim, d_a, d_b, d_b, d_model, d_model)
    dtypes = (F32, F32, BF16, F32, F32, F32)
    qkva, za, qb, zb, ga, gb, small, logf_t, crow, k_aug, q_aug, kt, vt, vtb, kb, conv_tail = _inproj(
        x2d, norm_g, wbig, wsmall, wsmall_t, wkv_t, conv0, conv_w, b_f, widths, dtypes, tm, nb, t_valid, prev_kv)
    oa, s_fin = _delta(qkva, small, za, s0, a_log, dt_bias, onorm_g, nb, t, lb, t_valid, nh_b)
    ob, ob_gated = attend(qb, kb, kt, vt, vtb, crow, k_aug, q_aug, zb)
    y = _merge(x2d, oa, ob, None if ob_gated else zb, ga, gb, w_pa.astype(BF16), w_pb.astype(BF16),
               w_o.astype(BF16), final_g, final_norm, next(r for r in (MERGE_ROWS, tm) if x2d.shape[0] % r == 0))
    return y, (kt, vt, logf_t, s_fin, conv_tail)


def kernel(x_prompt, x_sample, cache_k, cache_v, cache_logf, state_delta, state_conv, page_table, norm_g, w_in,
           conv_w, A_log, dt_bias, onorm_g, b_f, w_pa, w_pb, w_o, final_g):
    depth = norm_g.shape[0]
    nbp, tp_len, d_model = x_prompt.shape
    nbs, ts_len, _ = x_sample.shape
    nh_a, dk, dv = state_delta.shape[2:]
    conv_k, conv_dim = conv_w.shape[1:]
    nh_b, dh_b = cache_k.shape[3:]
    d_b = nh_b * dh_b
    ts_pad = LANES
    assert ts_len <= CHUNK and tp_len % LANES == 0 and conv_k - 1 <= SUBLANES and ts_len >= conv_k - 1

    hp = x_prompt.reshape(nbp * tp_len, d_model)
    hs = jnp.pad(x_sample, ((0, 0), (0, ts_pad - ts_len), (0, 0))).reshape(nbs * ts_pad, d_model)
    conv0_p = jnp.zeros((nbp, SUBLANES, conv_dim), F32)
    s0_p = jnp.zeros((nbp, nh_a, dk, dv), F32)
    n_pages = page_table.shape[1]
    page = cache_k.shape[2]
    assert dk == LANES and dv == LANES
    cache_kt = jnp.transpose(cache_k, (0, 1, 3, 4, 2)).reshape(depth, cache_k.shape[1], d_b, page)
    cache_vt = jnp.transpose(cache_v, (0, 1, 3, 4, 2)).reshape(depth, cache_v.shape[1], d_b, page)
    cache_lft = jnp.transpose(cache_logf, (0, 1, 3, 2))
    pages = next(g for g in (SAMPLE_PAGES_PER_STEP, 8, 4, 2, 1) if n_pages % g == 0)
    tq = next(r for r in (512, 256, LANES) if tp_len % r == 0)
    lb_p = next(r for r in (512, 256, LANES) if tp_len % r == 0)
    tm_p = next(r for r in (512, 256, LANES) if tp_len % r == 0)
    tm_s = ts_pad

    st_p, st_s = [], []
    for l in range(depth):
        lw = (norm_g[l], w_in[l], conv_w[l], A_log[l], dt_bias[l], onorm_g[l], b_f[l], w_pa[l], w_pb[l], w_o[l])
        last = l == depth - 1

        def attend_p(qb, kb, kt, vt, vtb, crow, k_aug, q_aug, zb):
            del kt, vt, crow
            return _fox_prompt(qb, kb, vtb, k_aug, q_aug, zb, nb=nbp, t=tp_len, nh=nh_b, dh=dh_b, tq=tq), True

        prev_p = None if l == 0 else tuple(a if a.ndim == 4 else a[None] for a in st_p[-1][:2])
        hp, sp = _layer(hp, nbp, tp_len, None, conv0_p, s0_p, attend_p, lw, final_g, last, tm_p, lb_p, prev_p)

        conv0_s = jnp.pad(state_conv[l], ((0, 0), (SUBLANES - (conv_k - 1), 0), (0, 0)))

        def attend_s(qb, kb, kt, vt, vtb, crow, k_aug, q_aug, zb, l=l):
            del kb, vtb, k_aug, q_aug, zb
            return _fox_sample(qb, kt, vt, crow, cache_kt, cache_vt, cache_lft, page_table, l, nbs, ts_pad,
                               ts_len, pages), False

        hs, ss = _layer(hs, nbs, ts_pad, ts_len, conv0_s, state_delta[l], attend_s, lw, final_g, last,
                        tm_s, ts_pad, None)
        st_p.append(sp)
        st_s.append(ss)

    def seq_view(a, nb, t, keep):
        return a.reshape(nb, t, *a.shape[1:])[:, :keep]

    def token_major(feat_major, nb, t, keep):
        a = feat_major.reshape(depth, nb, nh_b, dh_b, t)[..., :keep]
        return jnp.transpose(a, (0, 1, 4, 2, 3))

    y_prompt = hp.reshape(nbp, tp_len, d_model)
    y_sample = seq_view(hs, nbs, ts_pad, ts_len)
    outs_p, outs_s = [], []
    for (st, nb, t, keep, outs) in ((st_p, nbp, tp_len, tp_len, outs_p), (st_s, nbs, ts_pad, ts_len, outs_s)):
        gathered = st[-1][0].ndim == 4 and st[-1][0].shape[0] == depth
        k_all = token_major(st[-1][0] if gathered else jnp.stack([s[0] for s in st]), nb, t, keep)
        v_all = token_major(st[-1][1] if gathered else jnp.stack([s[1] for s in st]), nb, t, keep)
        logf_all = jnp.transpose(jnp.stack([s[2] for s in st])[..., :keep], (0, 1, 3, 2))
        delta_all = jnp.stack([s[3] for s in st])
        conv_all = jnp.stack([s[4][:, SUBLANES - (conv_k - 1):] for s in st])
        outs.extend([k_all, v_all, logf_all, delta_all, conv_all])
    return (y_prompt, y_sample, *outs_p, *outs_s)
```

```python
import functools
import math

import jax
import jax.numpy as jnp
from jax import lax
from jax.experimental import pallas as pl
from jax.experimental.pallas import tpu as pltpu

F32 = jnp.float32
BF16 = jnp.bfloat16
EPS = 1e-6
CHUNK = 64
LANES = 128
SUBLANES = 8
BF16_SUBLANES = 16
SMALL_COLS = 16
VMEM_LIMIT = 56 * 1024 * 1024
FOX_HEADS_PER_PASS = 8
FOX_TK = 256
SAMPLE_PAGES_PER_STEP = 32
MERGE_ROWS = 1024
CONV_COLS = 256
BIAS_LANES = 6

_NT = (((1,), (1,)), ((), ()))
_TN = (((0,), (0,)), ((), ()))


def _bdot(a, b):
    return jnp.dot(a.astype(BF16), b.astype(BF16), preferred_element_type=F32)


def _bdot_nt(a, b):
    return lax.dot_general(a.astype(BF16), b.astype(BF16), _NT, preferred_element_type=F32)


def _bdot_tn(a, b):
    return lax.dot_general(a.astype(BF16), b.astype(BF16), _TN, preferred_element_type=F32)


def _split3(x):
    x1 = x.astype(BF16)
    r1 = x - x1.astype(F32)
    x2 = r1.astype(BF16)
    x3 = (r1 - x2.astype(F32)).astype(BF16)
    return x1, x2, x3


def _xdot(mask_bf16, x, dims=None):
    parts = _split3(x)
    if dims is None:
        return sum(jnp.dot(mask_bf16, p, preferred_element_type=F32) for p in parts)
    return sum(lax.dot_general(mask_bf16, p, dims, preferred_element_type=F32) for p in parts)


def _xdot_r(x, mask_bf16, dims=None):
    parts = _split3(x)
    if dims is None:
        return sum(jnp.dot(p, mask_bf16, preferred_element_type=F32) for p in parts)
    return sum(lax.dot_general(p, mask_bf16, dims, preferred_element_type=F32) for p in parts)


def _iota2(shape, dim):
    return lax.broadcasted_iota(jnp.int32, shape, dim)


def _params(sem):
    return pltpu.CompilerParams(dimension_semantics=sem, vmem_limit_bytes=VMEM_LIMIT)


def _inproj_body(x_ref, g_ref, w_ref, ws_ref, wst_ref, wkvt_ref, conv0_ref, cw_ref, bfr_ref, bfc_ref, *refs,
                 segs, nj, t_last, nh, n_prev):
    prev_refs, refs = (refs[:2], refs[2:]) if n_prev else ((), refs)
    out_refs, (xbuf, carry_c, carry_r) = refs[:-3], refs[-3:]
    i = pl.program_id(0)
    tm = x_ref.shape[0]
    pad = SUBLANES
    conv_k = cw_ref.shape[0]
    x = x_ref[...]
    h = (x * lax.rsqrt(jnp.mean(x * x, axis=-1, keepdims=True) + EPS)) * g_ref[...]
    hb = h.astype(BF16)

    @pl.when(i % nj == 0)
    def _():
        xbuf[0:pad, :] = conv0_ref[...]

    @pl.when(i % nj != 0)
    def _():
        xbuf[0:pad, :] = xbuf[tm:tm + pad, :]

    n_tail = 10
    plain = [(o_ref, lo, c0, min(CONV_COLS, hi - lo - c0))
             for (lo, hi), o_ref in zip(segs[1:], out_refs[1:-n_tail]) for c0 in range(0, hi - lo, CONV_COLS)]

    def project(o_ref, lo, c0, width):
        o_ref[:, c0:c0 + width] = jnp.dot(hb, w_ref[:, lo + c0:lo + c0 + width],
                                          preferred_element_type=F32).astype(o_ref.dtype)

    (lo, hi) = segs[0]
    n_conv = (hi - lo) // CONV_COLS
    per_conv = -(-len(plain) // n_conv)
    for blk in range(n_conv):
        c0 = blk * CONV_COLS
        cols = slice(c0, c0 + CONV_COLS)
        xbuf[pad:pad + tm, cols] = jnp.dot(hb, w_ref[:, lo + c0:lo + c0 + CONV_COLS], preferred_element_type=F32)
        for task in plain[blk * per_conv:(blk + 1) * per_conv]:
            project(*task)
        acc = None
        for tap in range(conv_k):
            start = pad - (conv_k - 1) + tap
            term = cw_ref[tap:tap + 1, cols] * xbuf[start:start + tm, cols]
            acc = term if acc is None else acc + term
        out_refs[0][:, cols] = jax.nn.silu(acc)
    sm_ref, logf_ref, crow_ref, kaug_ref, qaug_ref, kt_ref, vt_ref, vtb_ref, kb_ref, cst_ref = out_refs[-n_tail:]
    small = jnp.dot(hb, ws_ref[...], preferred_element_type=F32)
    small_t = lax.dot_general(wst_ref[...], hb, _NT, preferred_element_type=F32)
    sm_ref[...] = small

    @pl.when(i % nj == 0)
    def _():
        carry_c[...] = jnp.zeros_like(carry_c)
        carry_r[...] = jnp.zeros_like(carry_r)

    logf_c = jax.nn.log_sigmoid(small + bfr_ref[...])
    tril = (_iota2((tm, tm), 0) >= _iota2((tm, tm), 1)).astype(BF16)
    ccol = _xdot(tril, logf_c) + carry_c[...]
    carry_c[...] = ccol[tm - 1:tm, :]
    li = _iota2((LANES, LANES), 0)
    lj = _iota2((LANES, LANES), 1)
    lane = _iota2((1, LANES), 1)
    in_heads = lane < nh * BIAS_LANES
    half = BIAS_LANES // 2
    k_aug = jnp.where(in_heads & (lane % BIAS_LANES >= half), 1.0, 0.0)
    q_aug = jnp.where(in_heads & (lane % BIAS_LANES < half), 1.0, 0.0)
    for p, piece in enumerate(_split3(ccol)):
        to_k = ((lj == BIAS_LANES * li + p) & (li < nh)).astype(BF16)
        to_q = ((lj == BIAS_LANES * li + half + p) & (li < nh)).astype(BF16)
        k_aug = k_aug - jnp.dot(piece, to_k, preferred_element_type=F32)
        q_aug = q_aug + jnp.dot(piece, to_q, preferred_element_type=F32)
    kaug_ref[...] = k_aug.astype(BF16)
    qaug_ref[...] = q_aug.astype(BF16)
    logf_r = jax.nn.log_sigmoid(small_t[0:nh, :] + bfc_ref[...])
    logf_ref[...] = logf_r
    triu = (_iota2((tm, tm), 0) <= _iota2((tm, tm), 1)).astype(BF16)
    crow = _xdot_r(logf_r, triu) + carry_r[:, 0:1]
    crow_ref[...] = crow
    carry_r[...] = jnp.broadcast_to(crow[:, tm - 1:tm], carry_r.shape)

    d_b = vtb_ref.shape[0]
    kvt = lax.dot_general(wkvt_ref[...], hb, _NT, preferred_element_type=F32)
    if n_prev:
        kt_ref[0:n_prev] = prev_refs[0][...]
        vt_ref[0:n_prev] = prev_refs[1][...]
        kt_ref[n_prev] = kvt[:d_b]
        vt_ref[n_prev] = kvt[d_b:]
    else:
        kt_ref[...] = kvt[:d_b]
        vt_ref[...] = kvt[d_b:]
    vtb_ref[...] = kvt[d_b:].astype(BF16)
    kb_ref[...] = kvt[:d_b].T.astype(BF16)
    cst_ref[...] = xbuf[t_last:t_last + pad, :]


def _inproj(x2d, g, wbig, wsmall, wsmall_t, wkv_t, conv0, conv_w, b_f, widths, dtypes, tm, nb, t_valid, prev_kv):
    m, d = x2d.shape
    t = m // nb
    nj = t // tm
    d_b = wkv_t.shape[0] // 2
    conv_dim = conv_w.shape[1]
    nh = b_f.shape[0]
    bfr = jnp.zeros((1, LANES), F32).at[0, :nh].set(b_f)
    bfc = b_f.reshape(nh, 1)
    t_last = (t_valid if t_valid is not None else t) - (nj - 1) * tm
    assert 0 < t_last <= tm and (t_valid is None or nj == 1)
    offs = [0]
    for w in widths:
        offs.append(offs[-1] + w)
    segs = tuple((offs[i], offs[i + 1]) for i in range(len(widths)))
    const = lambda i: (0, 0)
    seq_t = pl.BlockSpec((None, d_b, tm), lambda i: (i // nj, 0, i % nj))
    seq_h = pl.BlockSpec((None, nh, tm), lambda i: (i // nj, 0, i % nj))
    seq_l = pl.BlockSpec((None, tm, LANES), lambda i: (i // nj, i % nj, 0))
    seq_c = pl.BlockSpec((None, SUBLANES, conv_dim), lambda i: (i // nj, 0, 0))
    n_prev = 0 if prev_kv is None else prev_kv[0].shape[0]
    if n_prev:
        kv_shape = jax.ShapeDtypeStruct((n_prev + 1, nb, d_b, t), F32)
        kv_spec = pl.BlockSpec((n_prev + 1, None, d_b, tm), lambda i: (0, i // nj, 0, i % nj))
        prev_spec = pl.BlockSpec((n_prev, None, d_b, tm), lambda i: (0, i // nj, 0, i % nj))
    else:
        kv_shape, kv_spec = jax.ShapeDtypeStruct((nb, d_b, t), F32), seq_t
    out_shape = [jax.ShapeDtypeStruct((m, w), dt) for w, dt in zip(widths, dtypes)]
    out_shape += [jax.ShapeDtypeStruct((m, LANES), F32),
                  jax.ShapeDtypeStruct((nb, nh, t), F32), jax.ShapeDtypeStruct((nb, nh, t), F32),
                  jax.ShapeDtypeStruct((nb, t, LANES), BF16), jax.ShapeDtypeStruct((nb, t, LANES), BF16),
                  kv_shape, kv_shape,
                  jax.ShapeDtypeStruct((nb, d_b, t), BF16), jax.ShapeDtypeStruct((m, d_b), BF16),
                  jax.ShapeDtypeStruct((nb, SUBLANES, conv_dim), F32)]
    out_specs = [pl.BlockSpec((tm, w), lambda i: (i, 0)) for w in widths]
    out_specs += [pl.BlockSpec((tm, LANES), lambda i: (i, 0)), seq_h, seq_h, seq_l, seq_l,
                  kv_spec, kv_spec, seq_t, pl.BlockSpec((tm, d_b), lambda i: (i, 0)), seq_c]
    in_specs = [pl.BlockSpec((tm, d), lambda i: (i, 0)),
                pl.BlockSpec((1, d), const),
                pl.BlockSpec(wbig.shape, const, pipeline_mode=pl.Buffered(1)),
                pl.BlockSpec(wsmall.shape, const, pipeline_mode=pl.Buffered(1)),
                pl.BlockSpec(wsmall_t.shape, const, pipeline_mode=pl.Buffered(1)),
                pl.BlockSpec(wkv_t.shape, const, pipeline_mode=pl.Buffered(1)),
                seq_c,
                pl.BlockSpec(conv_w.shape, const),
                pl.BlockSpec((1, LANES), const),
                pl.BlockSpec((nh, 1), const)]
    args = [x2d, g.reshape(1, d), wbig, wsmall, wsmall_t, wkv_t, conv0, conv_w, bfr, bfc]
    if n_prev:
        in_specs += [prev_spec, prev_spec]
        args += list(prev_kv)
    return pl.pallas_call(
        functools.partial(_inproj_body, segs=segs, nj=nj, t_last=t_last, nh=nh, n_prev=n_prev),
        grid=(m // tm,),
        in_specs=in_specs,
        out_specs=out_specs,
        out_shape=out_shape,
        scratch_shapes=[pltpu.VMEM((tm + 2 * SUBLANES, conv_dim), F32),
                        pltpu.VMEM((1, LANES), F32),
                        pltpu.VMEM((nh, LANES), F32)],
        compiler_params=_params(("arbitrary",)),
        name="inproj",
    )(*args)


def _delta_body(qkv_ref, sm_ref, za_ref, s0_ref, alog_ref, dtb_ref, og_ref,
                oa_ref, sfin_ref, s_scr, *, nh, dk, dv, t_valid, n_chunks, beta_lane):
    j = pl.program_id(1)
    lb = qkv_ref.shape[0]
    c = CHUNK

    @pl.when(j == 0)
    def _():
        s_scr[...] = s0_ref[...]

    y = qkv_ref[...]

    sm = sm_ref[...]
    beta_all = jax.nn.sigmoid(sm)
    g_all = -jnp.exp(alog_ref[...]) * jax.nn.softplus(sm + dtb_ref[...])
    if t_valid is not None:
        tok = j * lb + _iota2((lb, LANES), 0)
        valid = tok < t_valid
        beta_all = jnp.where(valid, beta_all, 0.0)
        g_all = jnp.where(valid, g_all, 0.0)

    ri = _iota2((c, c), 0)
    ci = _iota2((c, c), 1)
    tril_b = (ri >= ci).astype(BF16)
    lower = ri >= ci
    strict = ri > ci
    eye = (ri == ci).astype(F32)
    og = og_ref[...]
    qscale = dk ** -0.5
    kbase = nh * dk
    vbase = 2 * nh * dk
    a_lane = beta_lane + nh

    probs = [(ch, h) for ch in range(n_chunks) for h in range(nh)]
    strict_w = _iota2((c, LANES), 0) > _iota2((c, LANES), 1)
    qs, ks, vs, betas = {}, {}, {}, {}
    diffs, gcums = {}, {}
    for ch in range(n_chunks):
        r0 = ch * c
        cols = []
        for h in range(nh):
            q = y[r0:r0 + c, h * dk:(h + 1) * dk]
            k = y[r0:r0 + c, kbase + h * dk:kbase + (h + 1) * dk]
            qs[ch, h] = q * lax.rsqrt(jnp.sum(q * q, axis=-1, keepdims=True) + EPS) * qscale
            ks[ch, h] = k * lax.rsqrt(jnp.sum(k * k, axis=-1, keepdims=True) + EPS)
            vs[ch, h] = y[r0:r0 + c, vbase + h * dv:vbase + (h + 1) * dv]
            betas[ch, h] = beta_all[r0:r0 + c, beta_lane + h:beta_lane + h + 1]
            g = g_all[r0:r0 + c, a_lane + h:a_lane + h + 1]
            g_w = jnp.broadcast_to(g, (c, LANES))
            cols += [jnp.where(strict_w, g_w, 0.0), g_w]
        cum = _xdot(tril_b, jnp.concatenate(cols, axis=1))
        for h in range(nh):
            diffs[ch, h] = cum[:, 2 * h * LANES:2 * h * LANES + c]
            gcums[ch, h] = cum[:, (2 * h + 1) * LANES:(2 * h + 2) * LANES]

    decays, egcs, kbs, n_pows, t_invs = {}, {}, {}, {}, {}
    for p in probs:
        decays[p] = jnp.where(lower, jnp.exp(diffs[p]), 0.0)
        egcs[p] = jnp.exp(gcums[p])
        kbs[p] = ks[p] * betas[p]
    for p in probs:
        n_pows[p] = -jnp.where(strict, _bdot_nt(kbs[p], ks[p]) * decays[p], 0.0)
        t_invs[p] = eye + n_pows[p]
    span = 2
    while span < c:
        for p in probs:
            n_pows[p] = _bdot(n_pows[p], n_pows[p])
        for p in probs:
            t_invs[p] = t_invs[p] + _bdot(t_invs[p], n_pows[p])
        span *= 2
    us, ws, attns = {}, {}, {}
    for p in probs:
        uw = _bdot(t_invs[p], jnp.concatenate([vs[p] * betas[p], kbs[p] * egcs[p]], axis=1))
        us[p] = uw[:, :dv]
        ws[p] = uw[:, dv:]
    for p in probs:
        attns[p] = _bdot_nt(qs[p], ks[p]) * decays[p]

    for ch in range(n_chunks):
        r0 = ch * c
        s_olds = [s_scr[h] for h in range(nh)]
        wss = [_bdot(jnp.concatenate([ws[ch, h], qs[ch, h] * egcs[ch, h]], axis=0), s_olds[h]) for h in range(nh)]
        v_news = [us[ch, h] - wss[h][:c] for h in range(nh)]
        os_ = [wss[h][c:] + _bdot(attns[ch, h], v_news[h]) for h in range(nh)]
        for h in range(nh):
            gcum = gcums[ch, h]
            glast = gcum[c - 1:c, :]
            kd = ks[ch, h] * jnp.exp(glast - gcum)
            s_scr[h] = s_olds[h] * jnp.exp(glast[:, :dv]) + _bdot_tn(kd, v_news[h])
        for h in range(nh):
            o = os_[h]
            o = o * lax.rsqrt(jnp.mean(o * o, axis=-1, keepdims=True) + EPS) * og
            z = za_ref[r0:r0 + c, h * dv:(h + 1) * dv]
            oa_ref[r0:r0 + c, h * dv:(h + 1) * dv] = (o * jax.nn.silu(z)).astype(oa_ref.dtype)
    if n_chunks * c < lb:
        oa_ref[n_chunks * c:lb, :] = jnp.zeros((lb - n_chunks * c, oa_ref.shape[1]), oa_ref.dtype)

    @pl.when(j == pl.num_programs(1) - 1)
    def _():
        sfin_ref[...] = s_scr[...]


def _delta(qkva, small, za, s0, a_log, dt_bias, onorm_g, nb, t, lb, t_valid, beta_lane):
    nh, dk, dv = s0.shape[1:]
    conv_dim = qkva.shape[1]
    nj = t // lb
    a_lane = beta_lane + nh
    alog_row = jnp.zeros((1, LANES), F32).at[0, a_lane:a_lane + nh].set(a_log)
    dtb_row = jnp.zeros((1, LANES), F32).at[0, a_lane:a_lane + nh].set(dt_bias)
    const = lambda b, j: (0, 0)
    n_chunks = lb // CHUNK if (t_valid is None or nj > 1) else -(-t_valid // CHUNK)
    body = functools.partial(_delta_body, nh=nh, dk=dk, dv=dv, t_valid=t_valid,
                             n_chunks=n_chunks, beta_lane=beta_lane)
    return pl.pallas_call(
        body,
        grid=(nb, nj),
        in_specs=[pl.BlockSpec((lb, conv_dim), lambda b, j: (b * nj + j, 0)),
                  pl.BlockSpec((lb, LANES), lambda b, j: (b * nj + j, 0)),
                  pl.BlockSpec((lb, nh * dv), lambda b, j: (b * nj + j, 0)),
                  pl.BlockSpec((None, nh, dk, dv), lambda b, j: (b, 0, 0, 0)),
                  pl.BlockSpec((1, LANES), const),
                  pl.BlockSpec((1, LANES), const),
                  pl.BlockSpec((1, dv), const)],
        out_specs=[pl.BlockSpec((lb, nh * dv), lambda b, j: (b * nj + j, 0)),
                   pl.BlockSpec((None, nh, dk, dv), lambda b, j: (b, 0, 0, 0))],
        out_shape=[jax.ShapeDtypeStruct((nb * t, nh * dv), BF16),
                   jax.ShapeDtypeStruct((nb, nh, dk, dv), F32)],
        scratch_shapes=[pltpu.VMEM((nh, dk, dv), F32)],
        compiler_params=_params(("parallel", "arbitrary")),
        name="delta",
    )(qkva, small, za, s0, alog_row, dtb_row, onorm_g.reshape(1, dv))


def _is_pow2(x):
    return math.frexp(x)[0] == 0.5


def _fox_body(q_ref, k_ref, v_ref, kaug_ref, qaug_ref, zb_ref, ob_ref, *, nh, dh, tk, heads_per_pass):
    qi = pl.program_id(1)
    tq = q_ref.shape[0]
    scale = dh ** -0.5
    lane = _iota2((1, LANES), 1)
    frow = _iota2((LANES, 1), 0)
    per = LANES // dh
    krow = _iota2((tk, tq), 0)
    qcol = _iota2((tk, tq), 1)
    ones_rows = jnp.ones((BF16_SUBLANES, tk), BF16)
    q_aug = qaug_ref[...]

    for h0 in range(0, nh, heads_per_pass):
        heads = list(range(h0, h0 + heads_per_pass))
        gls, halves, qhs = [], [], []
        for h in heads:
            grp, hh = divmod(h, per)
            gl = slice(grp * LANES, (grp + 1) * LANES)
            in_head = (lane >= hh * dh) & (lane < (hh + 1) * dh)
            q2 = q_ref[:, gl] * jnp.asarray(scale, q_ref.dtype)
            gls.append(gl)
            halves.append(hh)
            qh = jnp.where(in_head, q2, jnp.zeros_like(q2))
            own_bias = (lane >= h * BIAS_LANES) & (lane < (h + 1) * BIAS_LANES)
            qb = jnp.where(own_bias, q_aug, jnp.zeros_like(q_aug))
            qhs.append(jnp.concatenate([qh, qb], axis=1))

        def step(jb, carry, diag=None):
            masked = diag is not None
            ks = pl.multiple_of(jb * tk, tk)
            k_aug = kaug_ref[pl.ds(ks, tk), :]
            ss = []
            for i, h in enumerate(heads):
                k2 = k_ref[pl.ds(ks, tk), gls[i]]
                ss.append(_bdot_nt(jnp.concatenate([k2, k_aug], axis=1), qhs[i]))
            ps, new = [], []
            for i, h in enumerate(heads):
                m, acc = carry[i]
                s = ss[i]
                if masked:
                    s = jnp.where(krow + diag * tk <= qcol, s, -jnp.inf)
                m_new = jnp.maximum(m, jnp.max(s, axis=0, keepdims=True))
                ps.append(jnp.exp(s - m_new).astype(BF16))
                new.append((m_new, acc * jnp.exp(m - m_new)))
            out = []
            for i, h in enumerate(heads):
                m_new, acc = new[i]
                v_ext = jnp.concatenate([v_ref[gls[i], pl.ds(ks, tk)], ones_rows], axis=0)
                out.append((m_new, acc + jnp.dot(v_ext, ps[i], preferred_element_type=F32)))
            return tuple(out)

        init = tuple((jnp.full((1, tq), -jnp.inf, F32), jnp.zeros((LANES + BF16_SUBLANES, tq), F32))
                     for _ in heads)
        per_q = tq // tk
        fin = lax.fori_loop(0, qi * per_q, step, init)
        for d in range(per_q):
            fin = step(qi * per_q + d, fin, diag=d)
        for g0 in range(0, heads_per_pass, per):
            o2 = None
            for i in range(g0, g0 + per):
                _, acc = fin[i]
                o = acc[:LANES] / acc[LANES:LANES + 1]
                keep = (frow >= halves[i] * dh) & (frow < (halves[i] + 1) * dh)
                o2 = o if o2 is None else jnp.where(keep, o, o2)
            gl = gls[g0]
            ob_ref[:, gl] = (o2.T * jax.nn.silu(zb_ref[:, gl])).astype(ob_ref.dtype)


def _fox_prompt(qb, kb, v3, k_aug, q_aug, zb, nb, t, nh, dh, tq):
    d_b = nh * dh
    nq = t // tq
    q3 = qb.reshape(nb, t, d_b)
    k3 = kb.reshape(nb, t, d_b)
    z3 = zb.reshape(nb, t, d_b)
    assert nh * BIAS_LANES <= LANES and v3.dtype == BF16 and _is_pow2(dh ** -0.5)
    out = pl.pallas_call(
        functools.partial(_fox_body, nh=nh, dh=dh, tk=min(FOX_TK, tq), heads_per_pass=FOX_HEADS_PER_PASS),
        grid=(nb, nq),
        in_specs=[pl.BlockSpec((None, tq, d_b), lambda b, i: (b, i, 0)),
                  pl.BlockSpec((None, t, d_b), lambda b, i: (b, 0, 0)),
                  pl.BlockSpec((None, d_b, t), lambda b, i: (b, 0, 0)),
                  pl.BlockSpec((None, t, LANES), lambda b, i: (b, 0, 0)),
                  pl.BlockSpec((None, tq, LANES), lambda b, i: (b, i, 0)),
                  pl.BlockSpec((None, tq, d_b), lambda b, i: (b, i, 0))],
        out_specs=pl.BlockSpec((None, tq, d_b), lambda b, i: (b, i, 0)),
        out_shape=jax.ShapeDtypeStruct((nb, t, d_b), BF16),
        compiler_params=_params(("parallel", "arbitrary")),
        name="fox_prompt",
    )(q3, k3, v3, k_aug, q_aug, z3)
    return out.reshape(nb * t, d_b)


def _fox_sample_body(pt_ref, q_ref, kn_ref, vn_ref, crow_ref, *refs, nh, dh, t_new, pages):
    del pt_ref
    k_refs = refs[0:pages]
    v_refs = refs[pages:2 * pages]
    lf_refs = refs[2 * pages:3 * pages]
    ob_ref = refs[3 * pages]
    qblk, m_scr, l_scr, acc_scr, carry_scr = refs[3 * pages + 1:]
    j = pl.program_id(1)
    d_b = nh * dh
    nrow = t_new * nh
    page = k_refs[0].shape[1]
    tp = crow_ref.shape[1]
    scale = dh ** -0.5
    head_mask = (_iota2((nh, d_b), 1) // dh) == _iota2((nh, d_b), 0)
    c_tiled = jnp.concatenate([crow_ref[...]] * t_new, axis=0)
    row_tok = _iota2((nrow, tp), 0) // nh
    lane_tok = _iota2((nrow, tp), 1)
    c_own = jnp.sum(jnp.where(lane_tok == row_tok, c_tiled, 0.0), axis=-1, keepdims=True)

    @pl.when(j == 0)
    def _():
        for t in range(t_new):
            qrow = jnp.broadcast_to(q_ref[t:t + 1, :].astype(F32), (nh, d_b))
            qblk[t * nh:(t + 1) * nh, :] = jnp.where(head_mask, qrow, 0.0)
        m_scr[...] = jnp.full(m_scr.shape, -jnp.inf, F32)
        l_scr[...] = jnp.zeros_like(l_scr)
        acc_scr[...] = jnp.zeros_like(acc_scr)
        carry_scr[...] = jnp.zeros_like(carry_scr)

    def update(s_tiles, v_blocks):
        tmax = [jnp.max(s, axis=-1, keepdims=True) for s in s_tiles]
        m_old = m_scr[...]
        m_new = functools.reduce(jnp.maximum, tmax, m_old)
        corr = jnp.exp(m_old - m_new)
        p_tiles = [jnp.exp(s - m_new) for s in s_tiles]
        psum = functools.reduce(lambda a, b: a + b, [jnp.sum(p, axis=-1, keepdims=True) for p in p_tiles])
        l_scr[...] = l_scr[...] * corr + psum
        pv = functools.reduce(lambda a, b: a + b, [_bdot_nt(p, v) for p, v in zip(p_tiles, v_blocks)])
        acc_scr[...] = acc_scr[...] * corr + pv
        m_scr[...] = m_new

    after = (_iota2((page, page), 0) > _iota2((page, page), 1)).astype(BF16)
    sums = jnp.concatenate([after, jnp.ones((page, page), BF16)], axis=1)
    lf = jnp.concatenate([lf_refs[g][...] for g in range(pages)], axis=0)
    cum = _xdot_r(lf, sums)
    run = carry_scr[...]
    biases = []
    for g in range(pages):
        biases.append(cum[g * nh:(g + 1) * nh, :page] + run)
        run = run + cum[g * nh:(g + 1) * nh, page:]
    carry_scr[...] = run

    qb = qblk[...].astype(BF16)
    raw = [jnp.dot(qb, k_refs[g][...].astype(BF16), preferred_element_type=F32) for g in range(pages)]
    s_tiles = [raw[g] * scale + jnp.concatenate([biases[g]] * t_new, axis=0) + c_own for g in range(pages)]
    update(s_tiles, [v_refs[g][...] for g in range(pages)])

    @pl.when(j == pl.num_programs(1) - 1)
    def _():
        s = jnp.dot(qb, kn_ref[...].astype(BF16), preferred_element_type=F32) * scale + c_own - c_tiled
        s = jnp.where(lane_tok <= row_tok, s, -jnp.inf)
        update([s], [vn_ref[...]])
        o = acc_scr[...] / l_scr[...]
        ob_ref[...] = jnp.zeros_like(ob_ref)
        for t in range(t_new):
            blk = jnp.where(head_mask, o[t * nh:(t + 1) * nh, :], 0.0)
            ob_ref[t:t + 1, :] = jnp.sum(blk, axis=0, keepdims=True)


def _fox_sample(qb, k3, v3, crow, cache_kt, cache_vt, cache_lft, page_table, layer, nb, tp, t_new, pages):
    d_b, page = cache_kt.shape[2:]
    nh = cache_lft.shape[2]
    dh = d_b // nh
    n_pages = page_table.shape[1]
    nj = n_pages // pages
    q3 = qb.reshape(nb, tp, d_b)

    def page_map(g):
        return lambda b, j, pt: (layer, pt[b, n_pages - 1 - (j * pages + g)], 0, 0)

    seq = lambda b, j, pt: (b, 0, 0)
    in_specs = [pl.BlockSpec((None, tp, d_b), seq),
                pl.BlockSpec((None, d_b, tp), seq),
                pl.BlockSpec((None, d_b, tp), seq),
                pl.BlockSpec((None, nh, tp), seq)]
    in_specs += [pl.BlockSpec((None, None, d_b, page), page_map(g)) for g in range(pages)]
    in_specs += [pl.BlockSpec((None, None, d_b, page), page_map(g)) for g in range(pages)]
    in_specs += [pl.BlockSpec((None, None, nh, page), page_map(g)) for g in range(pages)]
    nrow = t_new * nh
    grid_spec = pltpu.PrefetchScalarGridSpec(
        num_scalar_prefetch=1,
        grid=(nb, nj),
        in_specs=in_specs,
        out_specs=pl.BlockSpec((None, tp, d_b), seq),
        scratch_shapes=[pltpu.VMEM((nrow, d_b), F32),
                        pltpu.VMEM((nrow, 1), F32),
                        pltpu.VMEM((nrow, 1), F32),
                        pltpu.VMEM((nrow, d_b), F32),
                        pltpu.VMEM((nh, page), F32)],
    )
    out = pl.pallas_call(
        functools.partial(_fox_sample_body, nh=nh, dh=dh, t_new=t_new, pages=pages),
        grid_spec=grid_spec,
        out_shape=jax.ShapeDtypeStruct((nb, tp, d_b), F32),
        compiler_params=_params(("parallel", "arbitrary")),
        name="fox_sample",
    )(page_table, q3, k3, v3, crow, *([cache_kt] * pages), *([cache_vt] * pages), *([cache_lft] * pages))
    return out.reshape(nb * tp, d_b)


def _merge_body(x_ref, oa_ref, ob_ref, ga_ref, gb_ref, wpa_ref, wpb_ref, wo_ref, fg_ref, *rest, final_norm, gate_b):
    ob = ob_ref[...]
    if gate_b:
        zb_ref, y_ref = rest
        ob = ob * jax.nn.silu(zb_ref[...])
    else:
        (y_ref,) = rest
    ya = jnp.dot(oa_ref[...].astype(BF16), wpa_ref[...], preferred_element_type=F32)
    yb = jnp.dot(ob.astype(BF16), wpb_ref[...], preferred_element_type=F32)
    merged = jax.nn.sigmoid(ga_ref[...]) * ya + jax.nn.sigmoid(gb_ref[...]) * yb
    y = x_ref[...] + jnp.dot(merged.astype(BF16), wo_ref[...], preferred_element_type=F32)
    if final_norm:
        y = (y * lax.rsqrt(jnp.mean(y * y, axis=-1, keepdims=True) + EPS)) * fg_ref[...]
    y_ref[...] = y


def _merge(x2d, oa, ob, zb, ga, gb, wpa, wpb, wo, final_g, final_norm, tm):
    m, d = x2d.shape
    row = lambda w: pl.BlockSpec((tm, w), lambda i: (i, 0))
    const = lambda i: (0, 0)
    gate_b = zb is not None
    in_specs = [row(d), row(oa.shape[1]), row(ob.shape[1]), row(d), row(d),
                pl.BlockSpec(wpa.shape, const), pl.BlockSpec(wpb.shape, const), pl.BlockSpec(wo.shape, const),
                pl.BlockSpec((1, d), const)]
    args = [x2d, oa, ob, ga, gb, wpa, wpb, wo, final_g.reshape(1, d)]
    if gate_b:
        in_specs.append(row(zb.shape[1]))
        args.append(zb)
    return pl.pallas_call(
        functools.partial(_merge_body, final_norm=final_norm, gate_b=gate_b),
        grid=(m // tm,),
        in_specs=in_specs,
        out_specs=row(d),
        out_shape=jax.ShapeDtypeStruct((m, d), F32),
        compiler_params=_params(("parallel",)),
        name="merge",
    )(*args)


def _pack_in_weights(w_in_l, conv_dim, d_a, nh_a, d_b, nh_b, d_model):
    sizes = (conv_dim, d_a, nh_a, nh_a, 3 * d_b, nh_b, d_b, d_model, d_model)
    offs = [0]
    for s in sizes:
        offs.append(offs[-1] + s)
    col = lambda i: w_in_l[:, offs[i]:offs[i + 1]]
    qkv_b = col(4)
    wbig = jnp.concatenate([col(0), col(1), qkv_b[:, :d_b], col(6), col(7), col(8)], axis=1).astype(BF16)
    small = jnp.concatenate([col(5), col(2), col(3)], axis=1)
    wsmall = jnp.pad(small, ((0, 0), (0, LANES - small.shape[1]))).astype(BF16)
    wsmall_t = small.T.astype(BF16)
    wkv_t = qkv_b[:, d_b:].T.astype(BF16)
    return wbig, wsmall, wsmall_t, wkv_t


def _layer(x2d, nb, t, t_valid, conv0, s0, attend, lw, final_g, final_norm, tm, lb, prev_kv):
    (norm_g, w_in, conv_w, a_log, dt_bias, onorm_g, b_f, w_pa, w_pb, w_o) = lw
    d_model = x2d.shape[1]
    nh_a, dk, dv = s0.shape[1:]
    conv_dim = conv_w.shape[1]
    d_a = nh_a * dv
    nh_b = b_f.shape[0]
    d_b = w_pb.shape[0]
    wbig, wsmall, wsmall_t, wkv_t = _pack_in_weights(w_in, conv_dim, d_a, nh_a, d_b, nh_b, d_model)
    widths = (conv_dim, d_a, d_b, d_b, d_model, d_model)
    dtypes = (F32, F32, BF16, F32, F32, F32)
    qkva, za, qb, zb, ga, gb, small, logf_t, crow, k_aug, q_aug, kt, vt, vtb, kb, conv_tail = _inproj(
        x2d, norm_g, wbig, wsmall, wsmall_t, wkv_t, conv0, conv_w, b_f, widths, dtypes, tm, nb, t_valid, prev_kv)
    oa, s_fin = _delta(qkva, small, za, s0, a_log, dt_bias, onorm_g, nb, t, lb, t_valid, nh_b)
    ob, ob_gated = attend(qb, kb, kt, vt, vtb, crow, k_aug, q_aug, zb)
    y = _merge(x2d, oa, ob, None if ob_gated else zb, ga, gb, w_pa.astype(BF16), w_pb.astype(BF16),
               w_o.astype(BF16), final_g, final_norm, next(r for r in (MERGE_ROWS, tm) if x2d.shape[0] % r == 0))
    return y, (kt, vt, logf_t, s_fin, conv_tail)


def kernel(x_prompt, x_sample, cache_k, cache_v, cache_logf, state_delta, state_conv, page_table, norm_g, w_in,
           conv_w, A_log, dt_bias, onorm_g, b_f, w_pa, w_pb, w_o, final_g):
    depth = norm_g.shape[0]
    nbp, tp_len, d_model = x_prompt.shape
    nbs, ts_len, _ = x_sample.shape
    nh_a, dk, dv = state_delta.shape[2:]
    conv_k, conv_dim = conv_w.shape[1:]
    nh_b, dh_b = cache_k.shape[3:]
    d_b = nh_b * dh_b
    ts_pad = LANES
    assert ts_len <= CHUNK and tp_len % LANES == 0 and conv_k - 1 <= SUBLANES and ts_len >= conv_k - 1

    hp = x_prompt.reshape(nbp * tp_len, d_model)
    hs = jnp.pad(x_sample, ((0, 0), (0, ts_pad - ts_len), (0, 0))).reshape(nbs * ts_pad, d_model)
    conv0_p = jnp.zeros((nbp, SUBLANES, conv_dim), F32)
    s0_p = jnp.zeros((nbp, nh_a, dk, dv), F32)
    n_pages = page_table.shape[1]
    page = cache_k.shape[2]
    assert dk == LANES and dv == LANES
    cache_kt = jnp.transpose(cache_k, (0, 1, 3, 4, 2)).reshape(depth, cache_k.shape[1], d_b, page)
    cache_vt = jnp.transpose(cache_v, (0, 1, 3, 4, 2)).reshape(depth, cache_v.shape[1], d_b, page)
    cache_lft = jnp.transpose(cache_logf, (0, 1, 3, 2))
    pages = next(g for g in (SAMPLE_PAGES_PER_STEP, 8, 4, 2, 1) if n_pages % g == 0)
    tq = next(r for r in (512, 256, LANES) if tp_len % r == 0)
    lb_p = next(r for r in (512, 256, LANES) if tp_len % r == 0)
    tm_p = next(r for r in (512, 256, LANES) if tp_len % r == 0)
    tm_s = ts_pad

    st_p, st_s = [], []
    for l in range(depth):
        lw = (norm_g[l], w_in[l], conv_w[l], A_log[l], dt_bias[l], onorm_g[l], b_f[l], w_pa[l], w_pb[l], w_o[l])
        last = l == depth - 1

        def attend_p(qb, kb, kt, vt, vtb, crow, k_aug, q_aug, zb):
            del kt, vt, crow
            return _fox_prompt(qb, kb, vtb, k_aug, q_aug, zb, nb=nbp, t=tp_len, nh=nh_b, dh=dh_b, tq=tq), True

        prev_p = None if l == 0 else tuple(a if a.ndim == 4 else a[None] for a in st_p[-1][:2])
        hp, sp = _layer(hp, nbp, tp_len, None, conv0_p, s0_p, attend_p, lw, final_g, last, tm_p, lb_p, prev_p)

        conv0_s = jnp.pad(state_conv[l], ((0, 0), (SUBLANES - (conv_k - 1), 0), (0, 0)))

        def attend_s(qb, kb, kt, vt, vtb, crow, k_aug, q_aug, zb, l=l):
            del kb, vtb, k_aug, q_aug, zb
            return _fox_sample(qb, kt, vt, crow, cache_kt, cache_vt, cache_lft, page_table, l, nbs, ts_pad,
                               ts_len, pages), False

        hs, ss = _layer(hs, nbs, ts_pad, ts_len, conv0_s, state_delta[l], attend_s, lw, final_g, last,
                        tm_s, ts_pad, None)
        st_p.append(sp)
        st_s.append(ss)

    def seq_view(a, nb, t, keep):
        return a.reshape(nb, t, *a.shape[1:])[:, :keep]

    def token_major(feat_major, nb, t, keep):
        a = feat_major.reshape(depth, nb, nh_b, dh_b, t)[..., :keep]
        return jnp.transpose(a, (0, 1, 4, 2, 3))

    y_prompt = hp.reshape(nbp, tp_len, d_model)
    y_sample = seq_view(hs, nbs, ts_pad, ts_len)
    outs_p, outs_s = [], []
    for (st, nb, t, keep, outs) in ((st_p, nbp, tp_len, tp_len, outs_p), (st_s, nbs, ts_pad, ts_len, outs_s)):
        gathered = st[-1][0].ndim == 4 and st[-1][0].shape[0] == depth
        k_all = token_major(st[-1][0] if gathered else jnp.stack([s[0] for s in st]), nb, t, keep)
        v_all = token_major(st[-1][1] if gathered else jnp.stack([s[1] for s in st]), nb, t, keep)
        logf_all = jnp.transpose(jnp.stack([s[2] for s in st])[..., :keep], (0, 1, 3, 2))
        delta_all = jnp.stack([s[3] for s in st])
        conv_all = jnp.stack([s[4][:, SUBLANES - (conv_k - 1):] for s in st])
        outs.extend([k_all, v_all, logf_all, delta_all, conv_all])
    return (y_prompt, y_sample, *outs_p, *outs_s)
```

```python
import functools
import math

import jax
import jax.numpy as jnp
from jax import lax
from jax.experimental import pallas as pl
from jax.experimental.pallas import tpu as pltpu

F32 = jnp.float32
BF16 = jnp.bfloat16
EPS = 1e-6
CHUNK = 64
LANES = 128
SUBLANES = 8
BF16_SUBLANES = 16
SMALL_COLS = 16
VMEM_LIMIT = 56 * 1024 * 1024
FOX_HEADS_PER_PASS = 8
FOX_TK = 512
SAMPLE_PAGES_PER_STEP = 32
MERGE_ROWS = 1024
CONV_COLS = 256
BIAS_LANES = 6

_NT = (((1,), (1,)), ((), ()))
_TN = (((0,), (0,)), ((), ()))


def _bdot(a, b):
    return jnp.dot(a.astype(BF16), b.astype(BF16), preferred_element_type=F32)


def _bdot_nt(a, b):
    return lax.dot_general(a.astype(BF16), b.astype(BF16), _NT, preferred_element_type=F32)


def _bdot_tn(a, b):
    return lax.dot_general(a.astype(BF16), b.astype(BF16), _TN, preferred_element_type=F32)


def _split3(x):
    x1 = x.astype(BF16)
    r1 = x - x1.astype(F32)
    x2 = r1.astype(BF16)
    x3 = (r1 - x2.astype(F32)).astype(BF16)
    return x1, x2, x3


def _xdot(mask_bf16, x, dims=None):
    parts = _split3(x)
    if dims is None:
        return sum(jnp.dot(mask_bf16, p, preferred_element_type=F32) for p in parts)
    return sum(lax.dot_general(mask_bf16, p, dims, preferred_element_type=F32) for p in parts)


def _xdot_r(x, mask_bf16, dims=None):
    parts = _split3(x)
    if dims is None:
        return sum(jnp.dot(p, mask_bf16, preferred_element_type=F32) for p in parts)
    return sum(lax.dot_general(p, mask_bf16, dims, preferred_element_type=F32) for p in parts)


def _iota2(shape, dim):
    return lax.broadcasted_iota(jnp.int32, shape, dim)


def _params(sem):
    return pltpu.CompilerParams(dimension_semantics=sem, vmem_limit_bytes=VMEM_LIMIT)


def _inproj_body(x_ref, g_ref, w_ref, ws_ref, wst_ref, wkvt_ref, conv0_ref, cw_ref, bfr_ref, bfc_ref, *refs,
                 segs, nj, t_last, nh, n_prev):
    prev_refs, refs = (refs[:2], refs[2:]) if n_prev else ((), refs)
    out_refs, (xbuf, carry_c, carry_r) = refs[:-3], refs[-3:]
    i = pl.program_id(0)
    tm = x_ref.shape[0]
    pad = SUBLANES
    conv_k = cw_ref.shape[0]
    x = x_ref[...]
    h = (x * lax.rsqrt(jnp.mean(x * x, axis=-1, keepdims=True) + EPS)) * g_ref[...]
    hb = h.astype(BF16)

    @pl.when(i % nj == 0)
    def _():
        xbuf[0:pad, :] = conv0_ref[...]

    @pl.when(i % nj != 0)
    def _():
        xbuf[0:pad, :] = xbuf[tm:tm + pad, :]

    n_tail = 10
    plain = [(o_ref, lo, c0, min(CONV_COLS, hi - lo - c0))
             for (lo, hi), o_ref in zip(segs[1:], out_refs[1:-n_tail]) for c0 in range(0, hi - lo, CONV_COLS)]

    def project(o_ref, lo, c0, width):
        o_ref[:, c0:c0 + width] = jnp.dot(hb, w_ref[:, lo + c0:lo + c0 + width],
                                          preferred_element_type=F32).astype(o_ref.dtype)

    (lo, hi) = segs[0]
    n_conv = (hi - lo) // CONV_COLS
    per_conv = -(-len(plain) // n_conv)
    for blk in range(n_conv):
        c0 = blk * CONV_COLS
        cols = slice(c0, c0 + CONV_COLS)
        xbuf[pad:pad + tm, cols] = jnp.dot(hb, w_ref[:, lo + c0:lo + c0 + CONV_COLS], preferred_element_type=F32)
        for task in plain[blk * per_conv:(blk + 1) * per_conv]:
            project(*task)
        acc = None
        for tap in range(conv_k):
            start = pad - (conv_k - 1) + tap
            term = cw_ref[tap:tap + 1, cols] * xbuf[start:start + tm, cols]
            acc = term if acc is None else acc + term
        out_refs[0][:, cols] = jax.nn.silu(acc)
    sm_ref, logf_ref, crow_ref, kaug_ref, qaug_ref, kt_ref, vt_ref, vtb_ref, kb_ref, cst_ref = out_refs[-n_tail:]
    small = jnp.dot(hb, ws_ref[...], preferred_element_type=F32)
    small_t = lax.dot_general(wst_ref[...], hb, _NT, preferred_element_type=F32)
    sm_ref[...] = small

    @pl.when(i % nj == 0)
    def _():
        carry_c[...] = jnp.zeros_like(carry_c)
        carry_r[...] = jnp.zeros_like(carry_r)

    logf_c = jax.nn.log_sigmoid(small + bfr_ref[...])
    tril = (_iota2((tm, tm), 0) >= _iota2((tm, tm), 1)).astype(BF16)
    ccol = _xdot(tril, logf_c) + carry_c[...]
    carry_c[...] = ccol[tm - 1:tm, :]
    li = _iota2((LANES, LANES), 0)
    lj = _iota2((LANES, LANES), 1)
    lane = _iota2((1, LANES), 1)
    in_heads = lane < nh * BIAS_LANES
    half = BIAS_LANES // 2
    k_aug = jnp.where(in_heads & (lane % BIAS_LANES >= half), 1.0, 0.0)
    q_aug = jnp.where(in_heads & (lane % BIAS_LANES < half), 1.0, 0.0)
    for p, piece in enumerate(_split3(ccol)):
        to_k = ((lj == BIAS_LANES * li + p) & (li < nh)).astype(BF16)
        to_q = ((lj == BIAS_LANES * li + half + p) & (li < nh)).astype(BF16)
        k_aug = k_aug - jnp.dot(piece, to_k, preferred_element_type=F32)
        q_aug = q_aug + jnp.dot(piece, to_q, preferred_element_type=F32)
    kaug_ref[...] = k_aug.astype(BF16)
    qaug_ref[...] = q_aug.astype(BF16)
    logf_r = jax.nn.log_sigmoid(small_t[0:nh, :] + bfc_ref[...])
    logf_ref[...] = logf_r
    triu = (_iota2((tm, tm), 0) <= _iota2((tm, tm), 1)).astype(BF16)
    crow = _xdot_r(logf_r, triu) + carry_r[:, 0:1]
    crow_ref[...] = crow
    carry_r[...] = jnp.broadcast_to(crow[:, tm - 1:tm], carry_r.shape)

    d_b = vtb_ref.shape[0]
    kvt = lax.dot_general(wkvt_ref[...], hb, _NT, preferred_element_type=F32)
    if n_prev:
        kt_ref[0:n_prev] = prev_refs[0][...]
        vt_ref[0:n_prev] = prev_refs[1][...]
        kt_ref[n_prev] = kvt[:d_b]
        vt_ref[n_prev] = kvt[d_b:]
    else:
        kt_ref[...] = kvt[:d_b]
        vt_ref[...] = kvt[d_b:]
    vtb_ref[...] = kvt[d_b:].astype(BF16)
    kb_ref[...] = kvt[:d_b].T.astype(BF16)
    cst_ref[...] = xbuf[t_last:t_last + pad, :]


def _inproj(x2d, g, wbig, wsmall, wsmall_t, wkv_t, conv0, conv_w, b_f, widths, dtypes, tm, nb, t_valid, prev_kv):
    m, d = x2d.shape
    t = m // nb
    nj = t // tm
    d_b = wkv_t.shape[0] // 2
    conv_dim = conv_w.shape[1]
    nh = b_f.shape[0]
    bfr = jnp.zeros((1, LANES), F32).at[0, :nh].set(b_f)
    bfc = b_f.reshape(nh, 1)
    t_last = (t_valid if t_valid is not None else t) - (nj - 1) * tm
    assert 0 < t_last <= tm and (t_valid is None or nj == 1)
    offs = [0]
    for w in widths:
        offs.append(offs[-1] + w)
    segs = tuple((offs[i], offs[i + 1]) for i in range(len(widths)))
    const = lambda i: (0, 0)
    seq_t = pl.BlockSpec((None, d_b, tm), lambda i: (i // nj, 0, i % nj))
    seq_h = pl.BlockSpec((None, nh, tm), lambda i: (i // nj, 0, i % nj))
    seq_l = pl.BlockSpec((None, tm, LANES), lambda i: (i // nj, i % nj, 0))
    seq_c = pl.BlockSpec((None, SUBLANES, conv_dim), lambda i: (i // nj, 0, 0))
    n_prev = 0 if prev_kv is None else prev_kv[0].shape[0]
    if n_prev:
        kv_shape = jax.ShapeDtypeStruct((n_prev + 1, nb, d_b, t), F32)
        kv_spec = pl.BlockSpec((n_prev + 1, None, d_b, tm), lambda i: (0, i // nj, 0, i % nj))
        prev_spec = pl.BlockSpec((n_prev, None, d_b, tm), lambda i: (0, i // nj, 0, i % nj))
    else:
        kv_shape, kv_spec = jax.ShapeDtypeStruct((nb, d_b, t), F32), seq_t
    out_shape = [jax.ShapeDtypeStruct((m, w), dt) for w, dt in zip(widths, dtypes)]
    out_shape += [jax.ShapeDtypeStruct((m, LANES), F32),
                  jax.ShapeDtypeStruct((nb, nh, t), F32), jax.ShapeDtypeStruct((nb, nh, t), F32),
                  jax.ShapeDtypeStruct((nb, t, LANES), BF16), jax.ShapeDtypeStruct((nb, t, LANES), BF16),
                  kv_shape, kv_shape,
                  jax.ShapeDtypeStruct((nb, d_b, t), BF16), jax.ShapeDtypeStruct((m, d_b), BF16),
                  jax.ShapeDtypeStruct((nb, SUBLANES, conv_dim), F32)]
    out_specs = [pl.BlockSpec((tm, w), lambda i: (i, 0)) for w in widths]
    out_specs += [pl.BlockSpec((tm, LANES), lambda i: (i, 0)), seq_h, seq_h, seq_l, seq_l,
                  kv_spec, kv_spec, seq_t, pl.BlockSpec((tm, d_b), lambda i: (i, 0)), seq_c]
    in_specs = [pl.BlockSpec((tm, d), lambda i: (i, 0)),
                pl.BlockSpec((1, d), const),
                pl.BlockSpec(wbig.shape, const, pipeline_mode=pl.Buffered(1)),
                pl.BlockSpec(wsmall.shape, const, pipeline_mode=pl.Buffered(1)),
                pl.BlockSpec(wsmall_t.shape, const, pipeline_mode=pl.Buffered(1)),
                pl.BlockSpec(wkv_t.shape, const, pipeline_mode=pl.Buffered(1)),
                seq_c,
                pl.BlockSpec(conv_w.shape, const),
                pl.BlockSpec((1, LANES), const),
                pl.BlockSpec((nh, 1), const)]
    args = [x2d, g.reshape(1, d), wbig, wsmall, wsmall_t, wkv_t, conv0, conv_w, bfr, bfc]
    if n_prev:
        in_specs += [prev_spec, prev_spec]
        args += list(prev_kv)
    return pl.pallas_call(
        functools.partial(_inproj_body, segs=segs, nj=nj, t_last=t_last, nh=nh, n_prev=n_prev),
        grid=(m // tm,),
        in_specs=in_specs,
        out_specs=out_specs,
        out_shape=out_shape,
        scratch_shapes=[pltpu.VMEM((tm + 2 * SUBLANES, conv_dim), F32),
                        pltpu.VMEM((1, LANES), F32),
                        pltpu.VMEM((nh, LANES), F32)],
        compiler_params=_params(("arbitrary",)),
        name="inproj",
    )(*args)


def _delta_body(qkv_ref, sm_ref, za_ref, s0_ref, alog_ref, dtb_ref, og_ref,
                oa_ref, sfin_ref, s_scr, *, nh, dk, dv, t_valid, n_chunks, beta_lane):
    j = pl.program_id(1)
    lb = qkv_ref.shape[0]
    c = CHUNK

    @pl.when(j == 0)
    def _():
        s_scr[...] = s0_ref[...]

    y = qkv_ref[...]

    sm = sm_ref[...]
    beta_all = jax.nn.sigmoid(sm)
    g_all = -jnp.exp(alog_ref[...]) * jax.nn.softplus(sm + dtb_ref[...])
    if t_valid is not None:
        tok = j * lb + _iota2((lb, LANES), 0)
        valid = tok < t_valid
        beta_all = jnp.where(valid, beta_all, 0.0)
        g_all = jnp.where(valid, g_all, 0.0)

    ri = _iota2((c, c), 0)
    ci = _iota2((c, c), 1)
    tril_b = (ri >= ci).astype(BF16)
    lower = ri >= ci
    strict = ri > ci
    eye = (ri == ci).astype(F32)
    og = og_ref[...]
    qscale = dk ** -0.5
    kbase = nh * dk
    vbase = 2 * nh * dk
    a_lane = beta_lane + nh

    probs = [(ch, h) for ch in range(n_chunks) for h in range(nh)]
    strict_w = _iota2((c, LANES), 0) > _iota2((c, LANES), 1)
    qs, ks, vs, betas = {}, {}, {}, {}
    diffs, gcums = {}, {}
    for ch in range(n_chunks):
        r0 = ch * c
        cols = []
        for h in range(nh):
            q = y[r0:r0 + c, h * dk:(h + 1) * dk]
            k = y[r0:r0 + c, kbase + h * dk:kbase + (h + 1) * dk]
            qs[ch, h] = q * lax.rsqrt(jnp.sum(q * q, axis=-1, keepdims=True) + EPS) * qscale
            ks[ch, h] = k * lax.rsqrt(jnp.sum(k * k, axis=-1, keepdims=True) + EPS)
            vs[ch, h] = y[r0:r0 + c, vbase + h * dv:vbase + (h + 1) * dv]
            betas[ch, h] = beta_all[r0:r0 + c, beta_lane + h:beta_lane + h + 1]
            g = g_all[r0:r0 + c, a_lane + h:a_lane + h + 1]
            g_w = jnp.broadcast_to(g, (c, LANES))
            cols += [jnp.where(strict_w, g_w, 0.0), g_w]
        cum = _xdot(tril_b, jnp.concatenate(cols, axis=1))
        for h in range(nh):
            diffs[ch, h] = cum[:, 2 * h * LANES:2 * h * LANES + c]
            gcums[ch, h] = cum[:, (2 * h + 1) * LANES:(2 * h + 2) * LANES]

    decays, egcs, kbs, n_pows, t_invs = {}, {}, {}, {}, {}
    for p in probs:
        decays[p] = jnp.where(lower, jnp.exp(diffs[p]), 0.0)
        egcs[p] = jnp.exp(gcums[p])
        kbs[p] = ks[p] * betas[p]
    for p in probs:
        n_pows[p] = -jnp.where(strict, _bdot_nt(kbs[p], ks[p]) * decays[p], 0.0)
        t_invs[p] = eye + n_pows[p]
    span = 2
    while span < c:
        for p in probs:
            n_pows[p] = _bdot(n_pows[p], n_pows[p])
        for p in probs:
            t_invs[p] = t_invs[p] + _bdot(t_invs[p], n_pows[p])
        span *= 2
    us, ws, attns = {}, {}, {}
    for p in probs:
        uw = _bdot(t_invs[p], jnp.concatenate([vs[p] * betas[p], kbs[p] * egcs[p]], axis=1))
        us[p] = uw[:, :dv]
        ws[p] = uw[:, dv:]
    for p in probs:
        attns[p] = _bdot_nt(qs[p], ks[p]) * decays[p]

    for ch in range(n_chunks):
        r0 = ch * c
        s_olds = [s_scr[h] for h in range(nh)]
        wss = [_bdot(jnp.concatenate([ws[ch, h], qs[ch, h] * egcs[ch, h]], axis=0), s_olds[h]) for h in range(nh)]
        v_news = [us[ch, h] - wss[h][:c] for h in range(nh)]
        os_ = [wss[h][c:] + _bdot(attns[ch, h], v_news[h]) for h in range(nh)]
        for h in range(nh):
            gcum = gcums[ch, h]
            glast = gcum[c - 1:c, :]
            kd = ks[ch, h] * jnp.exp(glast - gcum)
            s_scr[h] = s_olds[h] * jnp.exp(glast[:, :dv]) + _bdot_tn(kd, v_news[h])
        for h in range(nh):
            o = os_[h]
            o = o * lax.rsqrt(jnp.mean(o * o, axis=-1, keepdims=True) + EPS) * og
            z = za_ref[r0:r0 + c, h * dv:(h + 1) * dv]
            oa_ref[r0:r0 + c, h * dv:(h + 1) * dv] = (o * jax.nn.silu(z)).astype(oa_ref.dtype)
    if n_chunks * c < lb:
        oa_ref[n_chunks * c:lb, :] = jnp.zeros((lb - n_chunks * c, oa_ref.shape[1]), oa_ref.dtype)

    @pl.when(j == pl.num_programs(1) - 1)
    def _():
        sfin_ref[...] = s_scr[...]


def _delta(qkva, small, za, s0, a_log, dt_bias, onorm_g, nb, t, lb, t_valid, beta_lane):
    nh, dk, dv = s0.shape[1:]
    conv_dim = qkva.shape[1]
    nj = t // lb
    a_lane = beta_lane + nh
    alog_row = jnp.zeros((1, LANES), F32).at[0, a_lane:a_lane + nh].set(a_log)
    dtb_row = jnp.zeros((1, LANES), F32).at[0, a_lane:a_lane + nh].set(dt_bias)
    const = lambda b, j: (0, 0)
    n_chunks = lb // CHUNK if (t_valid is None or nj > 1) else -(-t_valid // CHUNK)
    body = functools.partial(_delta_body, nh=nh, dk=dk, dv=dv, t_valid=t_valid,
                             n_chunks=n_chunks, beta_lane=beta_lane)
    return pl.pallas_call(
        body,
        grid=(nb, nj),
        in_specs=[pl.BlockSpec((lb, conv_dim), lambda b, j: (b * nj + j, 0)),
                  pl.BlockSpec((lb, LANES), lambda b, j: (b * nj + j, 0)),
                  pl.BlockSpec((lb, nh * dv), lambda b, j: (b * nj + j, 0)),
                  pl.BlockSpec((None, nh, dk, dv), lambda b, j: (b, 0, 0, 0)),
                  pl.BlockSpec((1, LANES), const),
                  pl.BlockSpec((1, LANES), const),
                  pl.BlockSpec((1, dv), const)],
        out_specs=[pl.BlockSpec((lb, nh * dv), lambda b, j: (b * nj + j, 0)),
                   pl.BlockSpec((None, nh, dk, dv), lambda b, j: (b, 0, 0, 0))],
        out_shape=[jax.ShapeDtypeStruct((nb * t, nh * dv), BF16),
                   jax.ShapeDtypeStruct((nb, nh, dk, dv), F32)],
        scratch_shapes=[pltpu.VMEM((nh, dk, dv), F32)],
        compiler_params=_params(("parallel", "arbitrary")),
        name="delta",
    )(qkva, small, za, s0, alog_row, dtb_row, onorm_g.reshape(1, dv))


def _is_pow2(x):
    return math.frexp(x)[0] == 0.5


def _fox_body(q_ref, k_ref, v_ref, kaug_ref, qaug_ref, zb_ref, ob_ref, *, nh, dh, tk, heads_per_pass):
    qi = pl.program_id(1)
    tq = q_ref.shape[0]
    scale = dh ** -0.5
    lane = _iota2((1, LANES), 1)
    frow = _iota2((LANES, 1), 0)
    per = LANES // dh
    krow = _iota2((tk, tq), 0)
    qcol = _iota2((tk, tq), 1)
    ones_rows = jnp.ones((BF16_SUBLANES, tk), BF16)
    q_aug = qaug_ref[...]

    for h0 in range(0, nh, heads_per_pass):
        heads = list(range(h0, h0 + heads_per_pass))
        gls, halves, qhs = [], [], []
        for h in heads:
            grp, hh = divmod(h, per)
            gl = slice(grp * LANES, (grp + 1) * LANES)
            in_head = (lane >= hh * dh) & (lane < (hh + 1) * dh)
            q2 = q_ref[:, gl] * jnp.asarray(scale, q_ref.dtype)
            gls.append(gl)
            halves.append(hh)
            qh = jnp.where(in_head, q2, jnp.zeros_like(q2))
            own_bias = (lane >= h * BIAS_LANES) & (lane < (h + 1) * BIAS_LANES)
            qb = jnp.where(own_bias, q_aug, jnp.zeros_like(q_aug))
            qhs.append(jnp.concatenate([qh, qb], axis=1))

        def step(jb, carry, diag=None):
            masked = diag is not None
            ks = pl.multiple_of(jb * tk, tk)
            k_aug = kaug_ref[pl.ds(ks, tk), :]
            ss = []
            for i, h in enumerate(heads):
                k2 = k_ref[pl.ds(ks, tk), gls[i]]
                ss.append(_bdot_nt(jnp.concatenate([k2, k_aug], axis=1), qhs[i]))
            ps, new = [], []
            for i, h in enumerate(heads):
                m, acc = carry[i]
                s = ss[i]
                if masked:
                    s = jnp.where(krow + diag * tk <= qcol, s, -jnp.inf)
                m_new = jnp.maximum(m, jnp.max(s, axis=0, keepdims=True))
                ps.append(jnp.exp(s - m_new).astype(BF16))
                new.append((m_new, acc * jnp.exp(m - m_new)))
            out = []
            for i, h in enumerate(heads):
                m_new, acc = new[i]
                v_ext = jnp.concatenate([v_ref[gls[i], pl.ds(ks, tk)], ones_rows], axis=0)
                out.append((m_new, acc + jnp.dot(v_ext, ps[i], preferred_element_type=F32)))
            return tuple(out)

        init = tuple((jnp.full((1, tq), -jnp.inf, F32), jnp.zeros((LANES + BF16_SUBLANES, tq), F32))
                     for _ in heads)
        per_q = tq // tk
        fin = lax.fori_loop(0, qi * per_q, step, init)
        for d in range(per_q):
            fin = step(qi * per_q + d, fin, diag=d)
        for g0 in range(0, heads_per_pass, per):
            o2 = None
            for i in range(g0, g0 + per):
                _, acc = fin[i]
                o = acc[:LANES] / acc[LANES:LANES + 1]
                keep = (frow >= halves[i] * dh) & (frow < (halves[i] + 1) * dh)
                o2 = o if o2 is None else jnp.where(keep, o, o2)
            gl = gls[g0]
            ob_ref[:, gl] = (o2.T * jax.nn.silu(zb_ref[:, gl])).astype(ob_ref.dtype)


def _fox_prompt(qb, kb, v3, k_aug, q_aug, zb, nb, t, nh, dh, tq):
    d_b = nh * dh
    nq = t // tq
    q3 = qb.reshape(nb, t, d_b)
    k3 = kb.reshape(nb, t, d_b)
    z3 = zb.reshape(nb, t, d_b)
    assert nh * BIAS_LANES <= LANES and v3.dtype == BF16 and _is_pow2(dh ** -0.5)
    out = pl.pallas_call(
        functools.partial(_fox_body, nh=nh, dh=dh, tk=min(FOX_TK, tq), heads_per_pass=FOX_HEADS_PER_PASS),
        grid=(nb, nq),
        in_specs=[pl.BlockSpec((None, tq, d_b), lambda b, i: (b, i, 0)),
                  pl.BlockSpec((None, t, d_b), lambda b, i: (b, 0, 0)),
                  pl.BlockSpec((None, d_b, t), lambda b, i: (b, 0, 0)),
                  pl.BlockSpec((None, t, LANES), lambda b, i: (b, 0, 0)),
                  pl.BlockSpec((None, tq, LANES), lambda b, i: (b, i, 0)),
                  pl.BlockSpec((None, tq, d_b), lambda b, i: (b, i, 0))],
        out_specs=pl.BlockSpec((None, tq, d_b), lambda b, i: (b, i, 0)),
        out_shape=jax.ShapeDtypeStruct((nb, t, d_b), BF16),
        compiler_params=_params(("parallel", "arbitrary")),
        name="fox_prompt",
    )(q3, k3, v3, k_aug, q_aug, z3)
    return out.reshape(nb * t, d_b)


def _fox_sample_body(pt_ref, q_ref, kn_ref, vn_ref, crow_ref, *refs, nh, dh, t_new, pages):
    del pt_ref
    k_refs = refs[0:pages]
    v_refs = refs[pages:2 * pages]
    lf_refs = refs[2 * pages:3 * pages]
    ob_ref = refs[3 * pages]
    qblk, m_scr, l_scr, acc_scr, carry_scr = refs[3 * pages + 1:]
    j = pl.program_id(1)
    d_b = nh * dh
    nrow = t_new * nh
    page = k_refs[0].shape[1]
    tp = crow_ref.shape[1]
    scale = dh ** -0.5
    head_mask = (_iota2((nh, d_b), 1) // dh) == _iota2((nh, d_b), 0)
    c_tiled = jnp.concatenate([crow_ref[...]] * t_new, axis=0)
    row_tok = _iota2((nrow, tp), 0) // nh
    lane_tok = _iota2((nrow, tp), 1)
    c_own = jnp.sum(jnp.where(lane_tok == row_tok, c_tiled, 0.0), axis=-1, keepdims=True)

    @pl.when(j == 0)
    def _():
        for t in range(t_new):
            qrow = jnp.broadcast_to(q_ref[t:t + 1, :].astype(F32), (nh, d_b))
            qblk[t * nh:(t + 1) * nh, :] = jnp.where(head_mask, qrow, 0.0)
        m_scr[...] = jnp.full(m_scr.shape, -jnp.inf, F32)
        l_scr[...] = jnp.zeros_like(l_scr)
        acc_scr[...] = jnp.zeros_like(acc_scr)
        carry_scr[...] = jnp.zeros_like(carry_scr)

    def update(s_tiles, v_blocks):
        tmax = [jnp.max(s, axis=-1, keepdims=True) for s in s_tiles]
        m_old = m_scr[...]
        m_new = functools.reduce(jnp.maximum, tmax, m_old)
        corr = jnp.exp(m_old - m_new)
        p_tiles = [jnp.exp(s - m_new) for s in s_tiles]
        psum = functools.reduce(lambda a, b: a + b, [jnp.sum(p, axis=-1, keepdims=True) for p in p_tiles])
        l_scr[...] = l_scr[...] * corr + psum
        pv = functools.reduce(lambda a, b: a + b, [_bdot_nt(p, v) for p, v in zip(p_tiles, v_blocks)])
        acc_scr[...] = acc_scr[...] * corr + pv
        m_scr[...] = m_new

    after = (_iota2((page, page), 0) > _iota2((page, page), 1)).astype(BF16)
    sums = jnp.concatenate([after, jnp.ones((page, page), BF16)], axis=1)
    lf = jnp.concatenate([lf_refs[g][...] for g in range(pages)], axis=0)
    cum = _xdot_r(lf, sums)
    run = carry_scr[...]
    biases = []
    for g in range(pages):
        biases.append(cum[g * nh:(g + 1) * nh, :page] + run)
        run = run + cum[g * nh:(g + 1) * nh, page:]
    carry_scr[...] = run

    qb = qblk[...].astype(BF16)
    raw = [jnp.dot(qb, k_refs[g][...].astype(BF16), preferred_element_type=F32) for g in range(pages)]
    s_tiles = [raw[g] * scale + jnp.concatenate([biases[g]] * t_new, axis=0) + c_own for g in range(pages)]
    update(s_tiles, [v_refs[g][...] for g in range(pages)])

    @pl.when(j == pl.num_programs(1) - 1)
    def _():
        s = jnp.dot(qb, kn_ref[...].astype(BF16), preferred_element_type=F32) * scale + c_own - c_tiled
        s = jnp.where(lane_tok <= row_tok, s, -jnp.inf)
        update([s], [vn_ref[...]])
        o = acc_scr[...] / l_scr[...]
        ob_ref[...] = jnp.zeros_like(ob_ref)
        for t in range(t_new):
            blk = jnp.where(head_mask, o[t * nh:(t + 1) * nh, :], 0.0)
            ob_ref[t:t + 1, :] = jnp.sum(blk, axis=0, keepdims=True)


def _fox_sample(qb, k3, v3, crow, cache_kt, cache_vt, cache_lft, page_table, layer, nb, tp, t_new, pages):
    d_b, page = cache_kt.shape[2:]
    nh = cache_lft.shape[2]
    dh = d_b // nh
    n_pages = page_table.shape[1]
    nj = n_pages // pages
    q3 = qb.reshape(nb, tp, d_b)

    def page_map(g):
        return lambda b, j, pt: (layer, pt[b, n_pages - 1 - (j * pages + g)], 0, 0)

    seq = lambda b, j, pt: (b, 0, 0)
    in_specs = [pl.BlockSpec((None, tp, d_b), seq),
                pl.BlockSpec((None, d_b, tp), seq),
                pl.BlockSpec((None, d_b, tp), seq),
                pl.BlockSpec((None, nh, tp), seq)]
    in_specs += [pl.BlockSpec((None, None, d_b, page), page_map(g)) for g in range(pages)]
    in_specs += [pl.BlockSpec((None, None, d_b, page), page_map(g)) for g in range(pages)]
    in_specs += [pl.BlockSpec((None, None, nh, page), page_map(g)) for g in range(pages)]
    nrow = t_new * nh
    grid_spec = pltpu.PrefetchScalarGridSpec(
        num_scalar_prefetch=1,
        grid=(nb, nj),
        in_specs=in_specs,
        out_specs=pl.BlockSpec((None, tp, d_b), seq),
        scratch_shapes=[pltpu.VMEM((nrow, d_b), F32),
                        pltpu.VMEM((nrow, 1), F32),
                        pltpu.VMEM((nrow, 1), F32),
                        pltpu.VMEM((nrow, d_b), F32),
                        pltpu.VMEM((nh, page), F32)],
    )
    out = pl.pallas_call(
        functools.partial(_fox_sample_body, nh=nh, dh=dh, t_new=t_new, pages=pages),
        grid_spec=grid_spec,
        out_shape=jax.ShapeDtypeStruct((nb, tp, d_b), F32),
        compiler_params=_params(("parallel", "arbitrary")),
        name="fox_sample",
    )(page_table, q3, k3, v3, crow, *([cache_kt] * pages), *([cache_vt] * pages), *([cache_lft] * pages))
    return out.reshape(nb * tp, d_b)


def _merge_body(x_ref, oa_ref, ob_ref, ga_ref, gb_ref, wpa_ref, wpb_ref, wo_ref, fg_ref, *rest, final_norm, gate_b):
    ob = ob_ref[...]
    if gate_b:
        zb_ref, y_ref = rest
        ob = ob * jax.nn.silu(zb_ref[...])
    else:
        (y_ref,) = rest
    ya = jnp.dot(oa_ref[...].astype(BF16), wpa_ref[...], preferred_element_type=F32)
    yb = jnp.dot(ob.astype(BF16), wpb_ref[...], preferred_element_type=F32)
    merged = jax.nn.sigmoid(ga_ref[...]) * ya + jax.nn.sigmoid(gb_ref[...]) * yb
    y = x_ref[...] + jnp.dot(merged.astype(BF16), wo_ref[...], preferred_element_type=F32)
    if final_norm:
        y = (y * lax.rsqrt(jnp.mean(y * y, axis=-1, keepdims=True) + EPS)) * fg_ref[...]
    y_ref[...] = y


def _merge(x2d, oa, ob, zb, ga, gb, wpa, wpb, wo, final_g, final_norm, tm):
    m, d = x2d.shape
    row = lambda w: pl.BlockSpec((tm, w), lambda i: (i, 0))
    const = lambda i: (0, 0)
    gate_b = zb is not None
    in_specs = [row(d), row(oa.shape[1]), row(ob.shape[1]), row(d), row(d),
                pl.BlockSpec(wpa.shape, const), pl.BlockSpec(wpb.shape, const), pl.BlockSpec(wo.shape, const),
                pl.BlockSpec((1, d), const)]
    args = [x2d, oa, ob, ga, gb, wpa, wpb, wo, final_g.reshape(1, d)]
    if gate_b:
        in_specs.append(row(zb.shape[1]))
        args.append(zb)
    return pl.pallas_call(
        functools.partial(_merge_body, final_norm=final_norm, gate_b=gate_b),
        grid=(m // tm,),
        in_specs=in_specs,
        out_specs=row(d),
        out_shape=jax.ShapeDtypeStruct((m, d), F32),
        compiler_params=_params(("parallel",)),
        name="merge",
    )(*args)


def _pack_in_weights(w_in_l, conv_dim, d_a, nh_a, d_b, nh_b, d_model):
    sizes = (conv_dim, d_a, nh_a, nh_a, 3 * d_b, nh_b, d_b, d_model, d_model)
    offs = [0]
    for s in sizes:
        offs.append(offs[-1] + s)
    col = lambda i: w_in_l[:, offs[i]:offs[i + 1]]
    qkv_b = col(4)
    wbig = jnp.concatenate([col(0), col(1), qkv_b[:, :d_b], col(6), col(7), col(8)], axis=1).astype(BF16)
    small = jnp.concatenate([col(5), col(2), col(3)], axis=1)
    wsmall = jnp.pad(small, ((0, 0), (0, LANES - small.shape[1]))).astype(BF16)
    wsmall_t = small.T.astype(BF16)
    wkv_t = qkv_b[:, d_b:].T.astype(BF16)
    return wbig, wsmall, wsmall_t, wkv_t


def _layer(x2d, nb, t, t_valid, conv0, s0, attend, lw, final_g, final_norm, tm, lb, prev_kv):
    (norm_g, w_in, conv_w, a_log, dt_bias, onorm_g, b_f, w_pa, w_pb, w_o) = lw
    d_model = x2d.shape[1]
    nh_a, dk, dv = s0.shape[1:]
    conv_dim = conv_w.shape[1]
    d_a = nh_a * dv
    nh_b = b_f.shape[0]
    d_b = w_pb.shape[0]
    wbig, wsmall, wsmall_t, wkv_t = _pack_in_weights(w_in, conv_dim, d_a, nh_a, d_b, nh_b, d_model)
    widths = (conv_dim, d_a, d_b, d_b, d_model, d_model)
    dtypes = (F32, F32, BF16, F32, F32, F32)
    qkva, za, qb, zb, ga, gb, small, logf_t, crow, k_aug, q_aug, kt, vt, vtb, kb, conv_tail = _inproj(
        x2d, norm_g, wbig, wsmall, wsmall_t, wkv_t, conv0, conv_w, b_f, widths, dtypes, tm, nb, t_valid, prev_kv)
    oa, s_fin = _delta(qkva, small, za, s0, a_log, dt_bias, onorm_g, nb, t, lb, t_valid, nh_b)
    ob, ob_gated = attend(qb, kb, kt, vt, vtb, crow, k_aug, q_aug, zb)
    y = _merge(x2d, oa, ob, None if ob_gated else zb, ga, gb, w_pa.astype(BF16), w_pb.astype(BF16),
               w_o.astype(BF16), final_g, final_norm, next(r for r in (MERGE_ROWS, tm) if x2d.shape[0] % r == 0))
    return y, (kt, vt, logf_t, s_fin, conv_tail)


def kernel(x_prompt, x_sample, cache_k, cache_v, cache_logf, state_delta, state_conv, page_table, norm_g, w_in,
           conv_w, A_log, dt_bias, onorm_g, b_f, w_pa, w_pb, w_o, final_g):
    depth = norm_g.shape[0]
    nbp, tp_len, d_model = x_prompt.shape
    nbs, ts_len, _ = x_sample.shape
    nh_a, dk, dv = state_delta.shape[2:]
    conv_k, conv_dim = conv_w.shape[1:]
    nh_b, dh_b = cache_k.shape[3:]
    d_b = nh_b * dh_b
    ts_pad = LANES
    assert ts_len <= CHUNK and tp_len % LANES == 0 and conv_k - 1 <= SUBLANES and ts_len >= conv_k - 1

    hp = x_prompt.reshape(nbp * tp_len, d_model)
    hs = jnp.pad(x_sample, ((0, 0), (0, ts_pad - ts_len), (0, 0))).reshape(nbs * ts_pad, d_model)
    conv0_p = jnp.zeros((nbp, SUBLANES, conv_dim), F32)
    s0_p = jnp.zeros((nbp, nh_a, dk, dv), F32)
    n_pages = page_table.shape[1]
    page = cache_k.shape[2]
    assert dk == LANES and dv == LANES
    cache_kt = jnp.transpose(cache_k, (0, 1, 3, 4, 2)).reshape(depth, cache_k.shape[1], d_b, page)
    cache_vt = jnp.transpose(cache_v, (0, 1, 3, 4, 2)).reshape(depth, cache_v.shape[1], d_b, page)
    cache_lft = jnp.transpose(cache_logf, (0, 1, 3, 2))
    pages = next(g for g in (SAMPLE_PAGES_PER_STEP, 8, 4, 2, 1) if n_pages % g == 0)
    tq = next(r for r in (512, 256, LANES) if tp_len % r == 0)
    lb_p = next(r for r in (512, 256, LANES) if tp_len % r == 0)
    tm_p = next(r for r in (512, 256, LANES) if tp_len % r == 0)
    tm_s = ts_pad

    st_p, st_s = [], []
    for l in range(depth):
        lw = (norm_g[l], w_in[l], conv_w[l], A_log[l], dt_bias[l], onorm_g[l], b_f[l], w_pa[l], w_pb[l], w_o[l])
        last = l == depth - 1

        def attend_p(qb, kb, kt, vt, vtb, crow, k_aug, q_aug, zb):
            del kt, vt, crow
            return _fox_prompt(qb, kb, vtb, k_aug, q_aug, zb, nb=nbp, t=tp_len, nh=nh_b, dh=dh_b, tq=tq), True

        prev_p = None if l == 0 else tuple(a if a.ndim == 4 else a[None] for a in st_p[-1][:2])
        hp, sp = _layer(hp, nbp, tp_len, None, conv0_p, s0_p, attend_p, lw, final_g, last, tm_p, lb_p, prev_p)

        conv0_s = jnp.pad(state_conv[l], ((0, 0), (SUBLANES - (conv_k - 1), 0), (0, 0)))

        def attend_s(qb, kb, kt, vt, vtb, crow, k_aug, q_aug, zb, l=l):
            del kb, vtb, k_aug, q_aug, zb
            return _fox_sample(qb, kt, vt, crow, cache_kt, cache_vt, cache_lft, page_table, l, nbs, ts_pad,
                               ts_len, pages), False

        hs, ss = _layer(hs, nbs, ts_pad, ts_len, conv0_s, state_delta[l], attend_s, lw, final_g, last,
                        tm_s, ts_pad, None)
        st_p.append(sp)
        st_s.append(ss)

    def seq_view(a, nb, t, keep):
        return a.reshape(nb, t, *a.shape[1:])[:, :keep]

    def token_major(feat_major, nb, t, keep):
        a = feat_major.reshape(depth, nb, nh_b, dh_b, t)[..., :keep]
        return jnp.transpose(a, (0, 1, 4, 2, 3))

    y_prompt = hp.reshape(nbp, tp_len, d_model)
    y_sample = seq_view(hs, nbs, ts_pad, ts_len)
    outs_p, outs_s = [], []
    for (st, nb, t, keep, outs) in ((st_p, nbp, tp_len, tp_len, outs_p), (st_s, nbs, ts_pad, ts_len, outs_s)):
        gathered = st[-1][0].ndim == 4 and st[-1][0].shape[0] == depth
        k_all = token_major(st[-1][0] if gathered else jnp.stack([s[0] for s in st]), nb, t, keep)
        v_all = token_major(st[-1][1] if gathered else jnp.stack([s[1] for s in st]), nb, t, keep)
        logf_all = jnp.transpose(jnp.stack([s[2] for s in st])[..., :keep], (0, 1, 3, 2))
        delta_all = jnp.stack([s[3] for s in st])
        conv_all = jnp.stack([s[4][:, SUBLANES - (conv_k - 1):] for s in st])
        outs.extend([k_all, v_all, logf_all, delta_all, conv_all])
    return (y_prompt, y_sample, *outs_p, *outs_s)
```
